```python
import jax, jax.numpy as jnp
from jax import lax
import numpy as np

D_MODEL = 2048
BATCH = 8
SEQ = 4096
DEPTH = 4

HEAD_DIM = 128
N_ATTN_HEADS = 12
ATTN_WIDTH = N_ATTN_HEADS * HEAD_DIM
CONV_WIDTH = D_MODEL - ATTN_WIDTH
CONV_GROUPS = 4
CONV_K = 3
IN_PROJ_WIDTH = 3 * ATTN_WIDTH + 3 * CONV_WIDTH
DILATION_PATTERNS = ((128, 1), (512, 4), (2048, 16))
ROPE_THETA = 10000.0
POOL_WINDOWS = (2, 4, 8, 16)
POOL_GROUP = D_MODEL // len(POOL_WINDOWS)
FFN_DIM = 5632
N_EXPERTS = 8
TOP_K = 2
EXPERT_DIM = 5632
N_EVEN = (DEPTH + 1) // 2
N_ODD = DEPTH // 2
RMS_EPS = 1e-5
NEG_INF = -1e30

kernel_name = "hybrid_dilated_attn_shortconv_pool_moe"


def rms_norm(x, g):
    xf = x.astype(jnp.float32)
    y = xf * lax.rsqrt(jnp.mean(xf * xf, axis=-1, keepdims=True) + RMS_EPS)
    return (y * g.astype(jnp.float32)).astype(x.dtype)


def rotary(t, positions):
    half = HEAD_DIM // 2
    inv_freq = jnp.power(ROPE_THETA, -jnp.arange(half, dtype=jnp.float32) / half)
    ang = positions.astype(jnp.float32)[..., None] * inv_freq
    cos = jnp.cos(ang)[:, :, None, :]
    sin = jnp.sin(ang)[:, :, None, :]
    tf = t.astype(jnp.float32)
    t1, t2 = tf[..., :half], tf[..., half:]
    return jnp.concatenate([t1 * cos - t2 * sin, t2 * cos + t1 * sin], axis=-1).astype(t.dtype)


def dilated_window_attention(q, k, v, window, dilation):
    b, s, h, dh = q.shape
    n_back = window // dilation
    blk = n_back
    sub_len = s // dilation
    n_blk = -(-sub_len // blk)
    pad_len = n_blk * blk - sub_len

    def to_strided(t):
        t = t.reshape(b, sub_len, dilation, h, dh).transpose(0, 2, 3, 1, 4)
        return jnp.pad(t, ((0, 0), (0, 0), (0, 0), (0, pad_len), (0, 0)))

    def key_windows(t):
        t = jnp.pad(to_strided(t), ((0, 0), (0, 0), (0, 0), (blk, 0), (0, 0)))
        t = t.reshape(b, dilation, h, n_blk + 1, blk, dh)
        return jnp.concatenate([t[:, :, :, :-1], t[:, :, :, 1:]], axis=4)

    qs = to_strided(q).reshape(b, dilation, h, n_blk, blk, dh).astype(jnp.float32)
    kw = key_windows(k).astype(jnp.float32)
    vw = key_windows(v).astype(jnp.float32)
    scores = jnp.einsum('bdhnqe,bdhnke->bdhnqk', qs, kw) * (dh ** -0.5)
    q_idx = jnp.arange(blk)[:, None]
    k_idx = jnp.arange(2 * blk)[None, :]
    dist = q_idx + blk - k_idx
    key_pos = (jnp.arange(n_blk)[:, None, None] - 1) * blk + k_idx[None]
    valid = (dist >= 0) & (dist <= n_back) & (key_pos >= 0)
    scores = jnp.where(valid, scores, NEG_INF)
    m = jnp.max(scores, axis=-1, keepdims=True)
    p = jnp.exp(scores - m)
    den = jnp.sum(p, axis=-1, keepdims=True)
    o = jnp.einsum('bdhnqk,bdhnke->bdhnqe', p, vw) / den
    lse = (m + jnp.log(den))[..., 0]
    o = o.reshape(b, dilation, h, n_blk * blk, dh)[:, :, :, :sub_len]
    o = o.transpose(0, 3, 1, 2, 4).reshape(b, s, h, dh)
    lse = lse.reshape(b, dilation, h, n_blk * blk)[..., :sub_len]
    lse = lse.transpose(0, 3, 1, 2).reshape(b, s, h)
    return o, lse


def mixture_of_dilations(q, k, v):
    outs, lses = [], []
    for window, dilation in DILATION_PATTERNS:
        o, lse = dilated_window_attention(q, k, v, window, dilation)
        outs.append(o)
        lses.append(lse)
    wts = jax.nn.softmax(jnp.stack(lses, axis=0), axis=0)
    return jnp.sum(wts[..., None] * jnp.stack(outs, axis=0), axis=0)


def short_conv(u, w):
    c = u.shape[-1]
    return lax.conv_general_dilated(
        u, w[:, None, :].astype(u.dtype), window_strides=(1,),
        padding=[(CONV_K - 1, 0)], dimension_numbers=('NWC', 'WIO', 'NWC'),
        feature_group_count=c)


def even_mixer(xn, positions, w_in, conv_w, w_out):
    b, s, _ = xn.shape
    proj = xn @ w_in
    a = ATTN_WIDTH
    c = CONV_WIDTH
    q, k, v, gate_b, gate_c, hv = jnp.split(
        proj, [a, 2 * a, 3 * a, 3 * a + c, 3 * a + 2 * c], axis=-1)
    q = rotary(q.reshape(b, s, N_ATTN_HEADS, HEAD_DIM), positions)
    k = rotary(k.reshape(b, s, N_ATTN_HEADS, HEAD_DIM), positions)
    v = v.reshape(b, s, N_ATTN_HEADS, HEAD_DIM)
    attn = mixture_of_dilations(q, k, v).reshape(b, s, ATTN_WIDTH).astype(xn.dtype)
    conv = gate_b * short_conv(gate_c * hv, conv_w)
    return jnp.concatenate([attn, conv], axis=-1) @ w_out


def pool_mixer(xn, pool_w, pool_scale):
    b, s, d = xn.shape
    xf = xn.astype(jnp.float32)
    cs = jnp.cumsum(xf, axis=1)
    count = jnp.arange(1, s + 1, dtype=jnp.float32)[None, :, None]
    groups = []
    for gi, w in enumerate(POOL_WINDOWS):
        c = cs[..., gi * POOL_GROUP:(gi + 1) * POOL_GROUP]
        c_prev = jnp.pad(c[:, :-w], ((0, 0), (w, 0), (0, 0)))
        mean = (c - c_prev) / jnp.minimum(count, w)
        groups.append(mean - xf[..., gi * POOL_GROUP:(gi + 1) * POOL_GROUP])
    mixed = jnp.stack(groups, axis=2)
    out = jnp.einsum('bsgc,gce->bsge', mixed, pool_w.astype(jnp.float32)).reshape(b, s, d)
    return (out * pool_scale.astype(jnp.float32)).astype(xn.dtype)


def swiglu(x, wg, wu, wd):
    return (jax.nn.silu(x @ wg) * (x @ wu)) @ wd


def moe_swiglu(xn, router_w, wg, wu, wd):
    logits = jnp.einsum('bsd,de->bse', xn.astype(jnp.float32), router_w.astype(jnp.float32))
    top_vals, top_idx = lax.top_k(logits, TOP_K)
    gates = jax.nn.softmax(top_vals, axis=-1)
    comb = jnp.sum(jax.nn.one_hot(top_idx, N_EXPERTS, dtype=jnp.float32) * gates[..., None], axis=-2)
    out = jnp.zeros_like(xn)
    for e in range(N_EXPERTS):
        out = out + comb[..., e:e + 1].astype(xn.dtype) * swiglu(xn, wg[e], wu[e], wd[e])
    return out


def _normal(key, shape, fan_in):
    return jax.random.normal(key, shape, jnp.float32) * (fan_in ** -0.5)


def _gain(key, shape):
    return 1.0 + 0.02 * jax.random.normal(key, shape, jnp.float32)


def setup_inputs(seed: int = 0) -> dict:
    key = jax.random.key(seed)
    ks = jax.random.split(key, 20)
    x = jax.random.normal(ks[0], (BATCH, SEQ, D_MODEL), jnp.float32)
    positions = (jnp.arange(SEQ, dtype=jnp.int32)[None, :]
                 + jax.random.randint(ks[1], (BATCH, 1), 0, 1024, dtype=jnp.int32))
    return {
        "x": x,
        "positions": positions,
        "norm_mix_even": _gain(ks[2], (N_EVEN, D_MODEL)),
        "w_in": _normal(ks[3], (N_EVEN, D_MODEL, IN_PROJ_WIDTH), D_MODEL),
        "conv_w": _normal(ks[4], (N_EVEN, CONV_K, CONV_WIDTH), CONV_K),
        "w_out": _normal(ks[5], (N_EVEN, D_MODEL, D_MODEL), D_MODEL),
        "norm_ffn_even": _gain(ks[6], (N_EVEN, D_MODEL)),
        "w_gate": _normal(ks[7], (N_EVEN, D_MODEL, FFN_DIM), D_MODEL),
        "w_up": _normal(ks[8], (N_EVEN, D_MODEL, FFN_DIM), D_MODEL),
        "w_down": _normal(ks[9], (N_EVEN, FFN_DIM, D_MODEL), FFN_DIM),
        "norm_mix_odd": _gain(ks[10], (N_ODD, D_MODEL)),
        "pool_w": _normal(ks[11], (N_ODD, len(POOL_WINDOWS), POOL_GROUP, POOL_GROUP), POOL_GROUP),
        "pool_scale": 1.0 + 0.1 * jax.random.normal(ks[12], (N_ODD, D_MODEL), jnp.float32),
        "norm_ffn_odd": _gain(ks[13], (N_ODD, D_MODEL)),
        "router_w": _normal(ks[14], (N_ODD, D_MODEL, N_EXPERTS), D_MODEL),
        "exp_w_gate": _normal(ks[15], (N_ODD, N_EXPERTS, D_MODEL, EXPERT_DIM), D_MODEL),
        "exp_w_up": _normal(ks[16], (N_ODD, N_EXPERTS, D_MODEL, EXPERT_DIM), D_MODEL),
        "exp_w_down": _normal(ks[17], (N_ODD, N_EXPERTS, EXPERT_DIM, D_MODEL), EXPERT_DIM),
        "final_norm": _gain(ks[18], (D_MODEL,)),
    }


def reference(x, positions, norm_mix_even, w_in, conv_w, w_out, norm_ffn_even,
              w_gate, w_up, w_down, norm_mix_odd, pool_w, pool_scale, norm_ffn_odd,
              router_w, exp_w_gate, exp_w_up, exp_w_down, final_norm):
    h = x
    for layer in range(DEPTH):
        i = layer // 2
        if layer % 2 == 0:
            h = h + even_mixer(rms_norm(h, norm_mix_even[i]), positions, w_in[i], conv_w[i], w_out[i])
            h = h + swiglu(rms_norm(h, norm_ffn_even[i]), w_gate[i], w_up[i], w_down[i])
        else:
            h = h + pool_mixer(rms_norm(h, norm_mix_odd[i]), pool_w[i], pool_scale[i])
            h = h + moe_swiglu(rms_norm(h, norm_ffn_odd[i]), router_w[i],
                               exp_w_gate[i], exp_w_up[i], exp_w_down[i])
    return rms_norm(h, final_norm)
```

```python
import functools

import jax
import jax.numpy as jnp
from jax import lax
from jax.experimental import pallas as pl
from jax.experimental.pallas import tpu as pltpu

F32 = jnp.float32
BF16 = jnp.bfloat16

HEAD_DIM = 128
N_HEADS = 12
ATTN_WIDTH = N_HEADS * HEAD_DIM
CONV_WIDTH = 512
CONV_K = 3
DILATIONS = ((128, 1), (512, 4), (2048, 16))
KEYS_BACK = 128
ROPE_THETA = 10000.0
POOL_WINDOWS = (2, 4, 8, 16)
N_EXPERTS = 8
RMS_EPS = 1e-5
NEG_INF = -1e30
LANES = 128
HALO = 16
VMEM_LIMIT = 56 * 1024 * 1024


def _params(*sem):
    return pltpu.CompilerParams(dimension_semantics=sem, vmem_limit_bytes=VMEM_LIMIT)


def _rms(x, g):
    return x * lax.rsqrt(jnp.mean(x * x, axis=-1, keepdims=True) + RMS_EPS) * g


def _rope_kernel(pos_ref, freq_ref, sign_ref, cos_ref, sin_ref):
    ang = pos_ref[...].astype(F32) * freq_ref[...]
    cos_ref[...] = jnp.cos(ang)
    sin_ref[...] = jnp.sin(ang) * sign_ref[...]


def _rope_tables(positions, tm=1024):
    n = positions.size
    half = HEAD_DIM // 2
    inv_freq = jnp.power(ROPE_THETA, -jnp.arange(half, dtype=F32) / half)
    freq = jnp.concatenate([inv_freq, inv_freq])[None, :]
    sign = jnp.concatenate([-jnp.ones((half,), F32), jnp.ones((half,), F32)])[None, :]
    row = pl.BlockSpec((tm, HEAD_DIM), lambda i: (i, 0))
    const = pl.BlockSpec((1, HEAD_DIM), lambda i: (0, 0))
    return pl.pallas_call(
        _rope_kernel,
        grid=(n // tm,),
        in_specs=[pl.BlockSpec((tm, 1), lambda i: (i, 0)), const, const],
        out_specs=[row, row],
        out_shape=[jax.ShapeDtypeStruct((n, HEAD_DIM), F32)] * 2,
        compiler_params=_params("parallel"),
        name="rope_tables",
    )(positions.reshape(n, 1), freq, sign)


def _inproj_kernel(h_ref, g_ref, w_ref, cos_ref, sin_ref, o_ref, xn_ref, *, n_rot_tiles, tn):
    j = pl.program_id(1)

    @pl.when(j == 0)
    def _():
        xn_ref[...] = _rms(h_ref[...], g_ref[...]).astype(BF16)

    y = jnp.dot(xn_ref[...], w_ref[...], preferred_element_type=F32)

    @pl.when(j < n_rot_tiles)
    def _():
        cos = cos_ref[...]
        sin = sin_ref[...]
        for c in range(tn // HEAD_DIM):
            t = y[:, c * HEAD_DIM:(c + 1) * HEAD_DIM]
            rot = t * cos + pltpu.roll(t, HEAD_DIM // 2, axis=1) * sin
            o_ref[:, c * HEAD_DIM:(c + 1) * HEAD_DIM] = rot.astype(BF16)

    @pl.when(j >= n_rot_tiles)
    def _():
        o_ref[...] = y.astype(BF16)


def _inproj(h, g, w, cos, sin, tm=1024, tn=768):
    n, d = h.shape
    width = w.shape[1]
    assert (2 * ATTN_WIDTH) % tn == 0 and width % tn == 0 and n % tm == 0
    kern = functools.partial(_inproj_kernel, n_rot_tiles=2 * ATTN_WIDTH // tn, tn=tn)
    return pl.pallas_call(
        kern,
        grid=(n // tm, width // tn),
        in_specs=[
            pl.BlockSpec((tm, d), lambda i, j: (i, 0)),
            pl.BlockSpec((1, d), lambda i, j: (0, 0)),
            pl.BlockSpec((d, tn), lambda i, j: (0, j)),
            pl.BlockSpec((tm, HEAD_DIM), lambda i, j: (i, 0)),
            pl.BlockSpec((tm, HEAD_DIM), lambda i, j: (i, 0)),
        ],
        out_specs=pl.BlockSpec((tm, tn), lambda i, j: (i, j)),
        out_shape=jax.ShapeDtypeStruct((n, width), BF16),
        scratch_shapes=[pltpu.VMEM((tm, d), BF16)],
        compiler_params=_params("parallel", "arbitrary"),
        name="norm_inproj_rotary",
    )(h, g, w, cos, sin)


def _attn_kernel(q_ref, k_ref, v_ref, o_ref, qf, kf, vf, acc, m_s, l_s, *, seq):
    qf[...] = q_ref[...].astype(F32)
    kf[...] = k_ref[...].astype(F32)
    vf[...] = v_ref[...].astype(F32)
    blk = KEYS_BACK
    row = lax.broadcasted_iota(jnp.int32, (blk, blk), 0)
    col = lax.broadcasted_iota(jnp.int32, (blk, blk), 1)
    mask_cur = col <= row
    mask_prev = col >= row
    scale = HEAD_DIM ** -0.5
    last = len(DILATIONS) - 1
    nt = (((1,), (1,)), ((), ()))

    for gi, (window, d) in enumerate(DILATIONS):
        assert window // d == blk
        n_blk = seq // d // blk

        def rows(start, d=d):
            return pl.ds(start, blk, stride=d) if d > 1 else pl.ds(start, blk)

        def block(start, has_prev, gi=gi, d=d, rows=rows):
            idx = rows(start)
            qb = qf[idx, :].astype(BF16)
            kc = kf[idx, :].astype(BF16)
            vc = vf[idx, :].astype(BF16)
            s_c = lax.dot_general(qb, kc, nt, preferred_element_type=F32) * scale
            s_c = jnp.where(mask_cur, s_c, NEG_INF)
            s_top = s_c
            if has_prev:
                idp = rows(start - blk * d)
                kp = kf[idp, :].astype(BF16)
                vp = vf[idp, :].astype(BF16)
                s_p = lax.dot_general(qb, kp, nt, preferred_element_type=F32) * scale
                s_p = jnp.where(mask_prev, s_p, NEG_INF)
                s_top = jnp.maximum(s_c, s_p)
            m_blk = jnp.max(s_top, axis=-1, keepdims=True)
            if gi == 0:
                m_new = jnp.broadcast_to(m_blk, (blk, blk))
            else:
                m_old = m_s[idx, :]
                m_new = jnp.maximum(m_old, m_blk)
            p_c = jnp.exp(s_c - m_new)
            p_all = p_c
            pv = jnp.dot(p_c.astype(BF16), vc, preferred_element_type=F32)
            if has_prev:
                p_p = jnp.exp(s_p - m_new)
                p_all = p_c + p_p
                pv = pv + jnp.dot(p_p.astype(BF16), vp, preferred_element_type=F32)
            l_new = jnp.broadcast_to(jnp.sum(p_all, axis=-1, keepdims=True), (blk, blk))
            if gi > 0:
                alpha = jnp.exp(m_old - m_new)
                l_new = alpha * l_s[idx, :] + l_new
                pv = alpha * acc[idx, :] + pv
            if gi == last:
                acc[idx, :] = pv / l_new
            else:
                acc[idx, :] = pv
                m_s[idx, :] = m_new
                l_s[idx, :] = l_new

        if d == 1:
            block(0, False)

            def body(i, c, block=block):
                block(pl.multiple_of(i * blk, blk), True)
                return c

            lax.fori_loop(1, n_blk, body, 0)
        else:
            def rbody(r, c, block=block, d=d, n_blk=n_blk):
                block(r, False)

                def body(i, c2):
                    block(r + i * (blk * d), True)
                    return c2

                lax.fori_loop(1, n_blk, body, 0)
                return c

            lax.fori_loop(0, d, rbody, 0)

    o_ref[...] = acc[...].astype(BF16)


def _attention(proj, batch, seq):
    blockspec = lambda off: pl.BlockSpec((None, seq, HEAD_DIM), lambda b, h: (b, 0, off + h))
    return pl.pallas_call(
        functools.partial(_attn_kernel, seq=seq),
        grid=(batch, N_HEADS),
        in_specs=[blockspec(0), blockspec(N_HEADS), blockspec(2 * N_HEADS)],
        out_specs=blockspec(0),
        out_shape=jax.ShapeDtypeStruct((batch, seq, ATTN_WIDTH), BF16),
        scratch_shapes=[pltpu.VMEM((seq, HEAD_DIM), F32)] * 6,
        compiler_params=_params("parallel", "parallel"),
        name="dilated_attention",
    )(proj, proj, proj)


def _outproj_kernel(h_ref, a_ref, gb_ref, gc_ref, hv_ref, gch_ref, hvh_ref, cw_ref, wo_ref, o_ref,
                    ext, *, tm, seq):
    i = pl.program_id(0)
    first = (i * tm) % seq == 0
    u = gc_ref[...].astype(F32) * hv_ref[...].astype(F32)
    u_halo = gch_ref[...].astype(F32) * hvh_ref[...].astype(F32)
    ext[0:HALO, :] = jnp.where(first, 0.0, u_halo)
    ext[HALO:HALO + tm, :] = u
    w = cw_ref[...]
    conv = w[CONV_K - 1:CONV_K, :] * u
    for back in range(1, CONV_K):
        conv = conv + w[CONV_K - 1 - back:CONV_K - back, :] * ext[pl.ds(HALO - back, tm), :]
    c = (gb_ref[...].astype(F32) * conv).astype(BF16)
    y = jnp.dot(a_ref[...], wo_ref[0:ATTN_WIDTH, :], preferred_element_type=F32)
    y = y + jnp.dot(c, wo_ref[ATTN_WIDTH:, :], preferred_element_type=F32)
    o_ref[...] = h_ref[...] + y


def _outproj(h, attn, proj, conv_w, w_out, seq, tm=512):
    n, d = h.shape
    cb = 3 * ATTN_WIDTH // CONV_WIDTH
    hb = tm // HALO
    cur = lambda k: pl.BlockSpec((tm, CONV_WIDTH), lambda i: (i, cb + k))
    halo = lambda k: pl.BlockSpec((HALO, CONV_WIDTH), lambda i: (jnp.maximum(i * hb - 1, 0), cb + k))
    return pl.pallas_call(
        functools.partial(_outproj_kernel, tm=tm, seq=seq),
        grid=(n // tm,),
        in_specs=[
            pl.BlockSpec((tm, d), lambda i: (i, 0)),
            pl.BlockSpec((tm, ATTN_WIDTH), lambda i: (i, 0)),
            cur(0), cur(1), cur(2), halo(1), halo(2),
            pl.BlockSpec((CONV_K, CONV_WIDTH), lambda i: (0, 0)),
            pl.BlockSpec((d, d), lambda i: (0, 0)),
        ],
        out_specs=pl.BlockSpec((tm, d), lambda i: (i, 0)),
        out_shape=jax.ShapeDtypeStruct((n, d), F32),
        scratch_shapes=[pltpu.VMEM((HALO + tm, CONV_WIDTH), F32)],
        compiler_params=_params("parallel"),
        name="conv_outproj_residual",
    )(h, attn, proj, proj, proj, proj, proj, conv_w, w_out)


def _ffn_kernel(h_ref, g_ref, wg_ref, wu_ref, wd_ref, o_ref, xn_ref, acc_ref):
    j = pl.program_id(1)

    @pl.when(j == 0)
    def _():
        xn_ref[...] = _rms(h_ref[...], g_ref[...]).astype(BF16)
        acc_ref[...] = jnp.zeros_like(acc_ref)

    x = xn_ref[...]
    gate = jnp.dot(x, wg_ref[...], preferred_element_type=F32)
    up = jnp.dot(x, wu_ref[...], preferred_element_type=F32)
    act = (gate * jax.nn.sigmoid(gate) * up).astype(BF16)
    acc_ref[...] += jnp.dot(act, wd_ref[...], preferred_element_type=F32)

    @pl.when(j == pl.num_programs(1) - 1)
    def _():
        o_ref[...] = h_ref[...] + acc_ref[...]


def _ffn(h, g, wg, wu, wd, tm=512, tf=512):
    n, d = h.shape
    f = wg.shape[1]
    assert n % tm == 0 and f % tf == 0
    return pl.pallas_call(
        _ffn_kernel,
        grid=(n // tm, f // tf),
        in_specs=[
            pl.BlockSpec((tm, d), lambda i, j: (i, 0)),
            pl.BlockSpec((1, d), lambda i, j: (0, 0)),
            pl.BlockSpec((d, tf), lambda i, j: (0, j)),
            pl.BlockSpec((d, tf), lambda i, j: (0, j)),
            pl.BlockSpec((tf, d), lambda i, j: (j, 0)),
        ],
        out_specs=pl.BlockSpec((tm, d), lambda i, j: (i, 0)),
        out_shape=jax.ShapeDtypeStruct((n, d), F32),
        scratch_shapes=[pltpu.VMEM((tm, d), BF16), pltpu.VMEM((tm, d), F32)],
        compiler_params=_params("parallel", "arbitrary"),
        name="norm_swiglu_residual",
    )(h, g, wg, wu, wd)


ROUTE_E1, ROUTE_E2, ROUTE_R1, ROUTE_R2, ROUTE_G1, ROUTE_G2 = range(6)


def _pool_router_kernel(h_ref, hh_ref, gm_ref, pw_ref, ps_ref, gf_ref, rw_ref,
                        hn_ref, xn_ref, route_ref, cnt_ref, ext, run, *, tm, seq):
    i = pl.program_id(0)

    @pl.when(i == 0)
    def _():
        run[...] = jnp.zeros_like(run)

    pos0 = (i * tm) % seq
    h = h_ref[...]
    gm = gm_ref[...]
    xn = _rms(h, gm)
    ext[0:HALO, :] = jnp.where(pos0 == 0, 0.0, _rms(hh_ref[...], gm))
    ext[HALO:HALO + tm, :] = xn
    t1 = (pos0 + 1 + lax.broadcasted_iota(jnp.int32, (tm, 1), 0)).astype(F32)
    group = h.shape[1] // len(POOL_WINDOWS)
    outs = []
    for gi, w in enumerate(POOL_WINDOWS):
        cols = slice(gi * group, (gi + 1) * group)
        x_g = xn[:, cols]
        s = x_g
        for back in range(1, w):
            s = s + ext[pl.ds(HALO - back, tm), cols]
        mixed = s / jnp.minimum(t1, float(w)) - x_g
        outs.append(jnp.dot(mixed.astype(BF16), pw_ref[gi], preferred_element_type=F32))
    hn = h + jnp.concatenate(outs, axis=1) * ps_ref[...]
    hn_ref[...] = hn

    xn2 = _rms(hn, gf_ref[...])
    xn_ref[...] = xn2
    logits = jnp.dot(xn2, rw_ref[...], preferred_element_type=F32, precision=lax.Precision.HIGHEST)
    lane = lax.broadcasted_iota(jnp.int32, (tm, LANES), 1)
    logits = jnp.where(lane < N_EXPERTS, logits, -jnp.inf)
    m1 = jnp.max(logits, axis=-1, keepdims=True)
    e1 = jnp.min(jnp.where(logits == m1, lane, LANES), axis=-1, keepdims=True)
    rest = jnp.where(lane == e1, -jnp.inf, logits)
    m2 = jnp.max(rest, axis=-1, keepdims=True)
    e2 = jnp.min(jnp.where(rest == m2, lane, LANES), axis=-1, keepdims=True)
    ex = jnp.exp(m2 - m1)
    g1 = 1.0 / (1.0 + ex)
    g2 = ex / (1.0 + ex)

    hit1 = lane == e1
    hit2 = lane == e2
    onehot = jnp.where(hit1 | hit2, 1.0, 0.0)
    tri = (lax.broadcasted_iota(jnp.int32, (tm, tm), 1) < lax.broadcasted_iota(jnp.int32, (tm, tm), 0))
    before = jnp.dot(tri.astype(BF16), onehot.astype(BF16), preferred_element_type=F32) + run[...]
    r1 = jnp.sum(jnp.where(hit1, before, 0.0), axis=-1, keepdims=True)
    r2 = jnp.sum(jnp.where(hit2, before, 0.0), axis=-1, keepdims=True)
    run[...] = run[...] + jnp.sum(onehot, axis=0, keepdims=True)
    cnt_ref[...] = run[...]

    route = jnp.zeros((tm, LANES), F32)
    for slot, val in ((ROUTE_E1, e1.astype(F32)), (ROUTE_E2, e2.astype(F32)), (ROUTE_R1, r1),
                      (ROUTE_R2, r2), (ROUTE_G1, g1), (ROUTE_G2, g2)):
        route = jnp.where(lane == slot, val, route)
    route_ref[...] = route


def _pool_router(h, g_mix, pool_w, pool_scale, g_ffn, router_w, seq, tm=512):
    n, d = h.shape
    hb = tm // HALO
    row = pl.BlockSpec((tm, d), lambda i: (i, 0))
    vec = pl.BlockSpec((1, d), lambda i: (0, 0))
    rw = jnp.zeros((d, LANES), F32).at[:, :N_EXPERTS].set(router_w)
    return pl.pallas_call(
        functools.partial(_pool_router_kernel, tm=tm, seq=seq),
        grid=(n // tm,),
        in_specs=[
            row,
            pl.BlockSpec((HALO, d), lambda i: (jnp.maximum(i * hb - 1, 0), 0)),
            vec,
            pl.BlockSpec(pool_w.shape, lambda i: (0, 0, 0)),
            vec, vec,
            pl.BlockSpec((d, LANES), lambda i: (0, 0)),
        ],
        out_specs=[row, row, pl.BlockSpec((tm, LANES), lambda i: (i, 0)),
                   pl.BlockSpec((1, LANES), lambda i: (0, 0))],
        out_shape=[jax.ShapeDtypeStruct((n, d), F32), jax.ShapeDtypeStruct((n, d), F32),
                   jax.ShapeDtypeStruct((n, LANES), F32), jax.ShapeDtypeStruct((1, LANES), F32)],
        scratch_shapes=[pltpu.VMEM((HALO + tm, d), F32), pltpu.VMEM((1, LANES), F32)],
        compiler_params=_params("arbitrary"),
        name="pool_residual_router",
    )(h, h, g_mix, pool_w, pool_scale, g_ffn, rw)


def _row_copy(src_ref, src_row, dst_ref, dst_row, sem):
    return pltpu.make_async_copy(src_ref.at[pl.ds(src_row, 1)], dst_ref.at[pl.ds(dst_row, 1)], sem)


def _dispatch_kernel(d1_ref, d2_ref, x_ref, xs_in_ref, xs_ref, sem, *, tm):
    del xs_in_ref

    def issue(t, c):
        _row_copy(x_ref, t, xs_ref, d1_ref[0, t], sem).start()
        _row_copy(x_ref, t, xs_ref, d2_ref[0, t], sem).start()
        return c

    lax.fori_loop(0, tm, issue, 0)

    def drain(t, c):
        _row_copy(x_ref, 0, xs_ref, 0, sem).wait()
        _row_copy(x_ref, 0, xs_ref, 0, sem).wait()
        return c

    lax.fori_loop(0, tm, drain, 0)


def _dispatch(xn, dest1, dest2, n_rows, tm=512):
    n, d = xn.shape
    idx = pl.BlockSpec((None, 1, tm), lambda i: (i, 0, 0), memory_space=pltpu.SMEM)
    return pl.pallas_call(
        functools.partial(_dispatch_kernel, tm=tm),
        grid=(n // tm,),
        in_specs=[idx, idx, pl.BlockSpec((tm, d), lambda i: (i, 0)), pl.BlockSpec(memory_space=pl.ANY)],
        out_specs=pl.BlockSpec(memory_space=pl.ANY),
        out_shape=jax.ShapeDtypeStruct((n_rows, d), F32),
        scratch_shapes=[pltpu.SemaphoreType.DMA(())],
        input_output_aliases={3: 0},
        compiler_params=_params("arbitrary"),
        name="expert_dispatch",
    )(dest1.reshape(n // tm, 1, tm), dest2.reshape(n // tm, 1, tm), xn, jnp.zeros((n_rows, d), F32))


def _moe_kernel(te_ref, nt_ref, x_ref, wg_ref, wu_ref, wd_ref, o_ref, xb_ref, acc_ref):
    del te_ref
    i = pl.program_id(0)
    j = pl.program_id(1)
    active = i < nt_ref[0]

    @pl.when(active & (j == 0))
    def _():
        xb_ref[...] = x_ref[...].astype(BF16)
        acc_ref[...] = jnp.zeros_like(acc_ref)

    @pl.when(active)
    def _():
        x = xb_ref[...]
        gate = jnp.dot(x, wg_ref[...], preferred_element_type=F32)
        up = jnp.dot(x, wu_ref[...], preferred_element_type=F32)
        act = (gate * jax.nn.sigmoid(gate) * up).astype(BF16)
        acc_ref[...] += jnp.dot(act, wd_ref[...], preferred_element_type=F32)

    @pl.when(active & (j == pl.num_programs(1) - 1))
    def _():
        o_ref[...] = acc_ref[...]

    @pl.when(jnp.logical_not(active) & (j == 0))
    def _():
        o_ref[...] = jnp.zeros_like(o_ref)


def _moe(xs, tile_expert, n_tiles, wg, wu, wd, tm, tf=256):
    p, d = xs.shape
    f = wg.shape[2]
    nj = f // tf
    assert p % tm == 0 and f % tf == 0

    def tile(i, j, te, nt):
        return (jnp.minimum(i, nt[0] - 1), 0)

    def col(i, j, te, nt):
        return jnp.where(i < nt[0], j, nj - 1)

    grid_spec = pltpu.PrefetchScalarGridSpec(
        num_scalar_prefetch=2,
        grid=(p // tm, nj),
        in_specs=[
            pl.BlockSpec((tm, d), tile),
            pl.BlockSpec((None, d, tf), lambda i, j, te, nt: (te[i], 0, col(i, j, te, nt))),
            pl.BlockSpec((None, d, tf), lambda i, j, te, nt: (te[i], 0, col(i, j, te, nt))),
            pl.BlockSpec((None, tf, d), lambda i, j, te, nt: (te[i], col(i, j, te, nt), 0)),
        ],
        out_specs=pl.BlockSpec((tm, d), lambda i, j, te, nt: (i, 0)),
        scratch_shapes=[pltpu.VMEM((tm, d), BF16), pltpu.VMEM((tm, d), F32)],
    )
    return pl.pallas_call(
        _moe_kernel,
        grid_spec=grid_spec,
        out_shape=jax.ShapeDtypeStruct((p, d), F32),
        compiler_params=_params("arbitrary", "arbitrary"),
        name="grouped_expert_swiglu",
    )(tile_expert, n_tiles, xs, wg, wu, wd)


def _combine_kernel(d1_ref, d2_ref, h_ref, route_ref, gn_ref, ys_ref, o_ref, b1, b2, sem, *, tm, final_norm):
    def issue(t, c):
        _row_copy(ys_ref, d1_ref[0, t], b1, t, sem).start()
        _row_copy(ys_ref, d2_ref[0, t], b2, t, sem).start()
        return c

    lax.fori_loop(0, tm, issue, 0)

    def drain(t, c):
        _row_copy(ys_ref, 0, b1, 0, sem).wait()
        _row_copy(ys_ref, 0, b2, 0, sem).wait()
        return c

    lax.fori_loop(0, tm, drain, 0)

    route = route_ref[...]
    g1 = route[:, ROUTE_G1:ROUTE_G1 + 1]
    g2 = route[:, ROUTE_G2:ROUTE_G2 + 1]
    out = h_ref[...] + (g1 * b1[...] + g2 * b2[...])
    if final_norm:
        out = _rms(out, gn_ref[...])
    o_ref[...] = out


def _combine(h, route, dest1, dest2, ys, g_final, final_norm, tm=512):
    n, d = h.shape
    idx = pl.BlockSpec((None, 1, tm), lambda i: (i, 0, 0), memory_space=pltpu.SMEM)
    row = pl.BlockSpec((tm, d), lambda i: (i, 0))
    return pl.pallas_call(
        functools.partial(_combine_kernel, tm=tm, final_norm=final_norm),
        grid=(n // tm,),
        in_specs=[idx, idx, row, pl.BlockSpec((tm, LANES), lambda i: (i, 0)),
                  pl.BlockSpec((1, d), lambda i: (0, 0)), pl.BlockSpec(memory_space=pl.ANY)],
        out_specs=row,
        out_shape=jax.ShapeDtypeStruct((n, d), F32),
        scratch_shapes=[pltpu.VMEM((tm, d), F32), pltpu.VMEM((tm, d), F32), pltpu.SemaphoreType.DMA(())],
        compiler_params=_params("arbitrary"),
        name="expert_combine_residual",
    )(dest1.reshape(n // tm, 1, tm), dest2.reshape(n // tm, 1, tm), h, route, g_final, ys)


def _moe_layer(h, route, counts, xn, wg, wu, wd, g_final, final_norm, tm=512):
    n, d = h.shape
    n_tiles_max = 2 * n // tm + N_EXPERTS
    counts = counts[0, :N_EXPERTS].astype(jnp.int32)
    padded = (counts + tm - 1) // tm * tm
    ends = jnp.cumsum(padded)
    offs = ends - padded
    col = lambda k: route[:, k].astype(jnp.int32)
    dest1 = offs[col(ROUTE_E1)] + col(ROUTE_R1)
    dest2 = offs[col(ROUTE_E2)] + col(ROUTE_R2)
    n_tiles = (ends[-1] // tm).astype(jnp.int32)
    tile_start = jnp.minimum(jnp.arange(n_tiles_max, dtype=jnp.int32), n_tiles - 1) * tm
    tile_expert = jnp.minimum(jnp.searchsorted(ends, tile_start, side="right"), N_EXPERTS - 1).astype(jnp.int32)
    xs = _dispatch(xn, dest1, dest2, n_tiles_max * tm)
    ys = _moe(xs, tile_expert, n_tiles.reshape(1), wg, wu, wd, tm)
    return _combine(h, route, dest1, dest2, ys, g_final, final_norm)


def kernel(x, positions, norm_mix_even, w_in, conv_w, w_out, norm_ffn_even, w_gate, w_up, w_down,
           norm_mix_odd, pool_w, pool_scale, norm_ffn_odd, router_w, exp_w_gate, exp_w_up, exp_w_down,
           final_norm):
    batch, seq, d = x.shape
    n = batch * seq
    depth = norm_mix_even.shape[0] + norm_mix_odd.shape[0]
    assert depth % 2 == 0, "the final norm is fused into the last (odd) layer's combine"
    h = x.reshape(n, d)
    cos, sin = _rope_tables(positions)
    g_final = final_norm[None, :]
    for layer in range(depth):
        i = layer // 2
        if layer % 2 == 0:
            proj = _inproj(h, norm_mix_even[i][None, :], w_in[i].astype(BF16), cos, sin)
            attn = _attention(proj.reshape(batch, seq, -1), batch, seq).reshape(n, ATTN_WIDTH)
            h = _outproj(h, attn, proj, conv_w[i], w_out[i].astype(BF16), seq)
            h = _ffn(h, norm_ffn_even[i][None, :], w_gate[i].astype(BF16), w_up[i].astype(BF16),
                     w_down[i].astype(BF16))
        else:
            h, xn, route, counts = _pool_router(h, norm_mix_odd[i][None, :], pool_w[i].astype(BF16),
                                                pool_scale[i][None, :], norm_ffn_odd[i][None, :],
                                                router_w[i], seq)
            h = _moe_layer(h, route, counts, xn, exp_w_gate[i].astype(BF16), exp_w_up[i].astype(BF16),
                           exp_w_down[i].astype(BF16), g_final, layer == depth - 1)
    return h.reshape(batch, seq, d)
```

```python
import functools
import math

import jax
import jax.numpy as jnp
from jax import lax
from jax.experimental import pallas as pl
from jax.experimental.pallas import tpu as pltpu

F32 = jnp.float32
BF16 = jnp.bfloat16

HEAD_DIM = 128
N_HEADS = 12
ATTN_WIDTH = N_HEADS * HEAD_DIM
CONV_WIDTH = 512
CONV_K = 3
DILATIONS = ((128, 1), (512, 4), (2048, 16))
KEYS_BACK = 128
ROPE_THETA = 10000.0
POOL_WINDOWS = (2, 4, 8, 16)
N_EXPERTS = 8
RMS_EPS = 1e-5
NEG_INF = -1e30
LANES = 128
HALO = 16
VMEM_LIMIT = 56 * 1024 * 1024
ATTN_TILE_BLOCKS = 4
ATTN_INTERLEAVE = 2
ROW_DMA_UNROLL = 8


def _params(*sem):
    return pltpu.CompilerParams(dimension_semantics=sem, vmem_limit_bytes=VMEM_LIMIT)


def _rms(x, g):
    return x * lax.rsqrt(jnp.mean(x * x, axis=-1, keepdims=True) + RMS_EPS) * g


def _pack_bf16_pairs(x):
    half = x.shape[1] // 2
    bits = lax.bitcast_convert_type(x.astype(BF16).astype(F32), jnp.uint32)
    return (bits[:, :half] >> 16) | (bits[:, half:] & jnp.uint32(0xFFFF0000))


def _unpack_bf16_pairs(words):
    lo = lax.bitcast_convert_type(words << 16, F32).astype(BF16)
    hi = lax.bitcast_convert_type(words & jnp.uint32(0xFFFF0000), F32).astype(BF16)
    return lo, hi


def _rope_kernel(pos_ref, freq_ref, sign_ref, cos_ref, sin_ref):
    ang = pos_ref[...].astype(F32) * freq_ref[...]
    cos_ref[...] = jnp.cos(ang)
    sin_ref[...] = jnp.sin(ang) * sign_ref[...]


def _rope_tables(positions, tm=1024):
    n = positions.size
    half = HEAD_DIM // 2
    inv_freq = jnp.power(ROPE_THETA, -jnp.arange(half, dtype=F32) / half)
    freq = jnp.concatenate([inv_freq, inv_freq])[None, :]
    sign = jnp.concatenate([-jnp.ones((half,), F32), jnp.ones((half,), F32)])[None, :]
    row = pl.BlockSpec((tm, HEAD_DIM), lambda i: (i, 0))
    const = pl.BlockSpec((1, HEAD_DIM), lambda i: (0, 0))
    return pl.pallas_call(
        _rope_kernel,
        grid=(n // tm,),
        in_specs=[pl.BlockSpec((tm, 1), lambda i: (i, 0)), const, const],
        out_specs=[row, row],
        out_shape=[jax.ShapeDtypeStruct((n, HEAD_DIM), F32)] * 2,
        compiler_params=_params("parallel"),
        name="rope_tables",
    )(positions.reshape(n, 1), freq, sign)


def _inproj_kernel(h_ref, g_ref, w_ref, cos_ref, sin_ref, o_ref, xn_ref, *, rotary, tn):
    j = pl.program_id(1)

    @pl.when(j == 0)
    def _():
        xn_ref[...] = _rms(h_ref[...], g_ref[...]).astype(BF16)

    y = jnp.dot(xn_ref[...], w_ref[...], preferred_element_type=F32)
    if rotary:
        cos = cos_ref[...]
        sin = sin_ref[...]
        for c in range(tn // HEAD_DIM):
            t = y[:, c * HEAD_DIM:(c + 1) * HEAD_DIM]
            rot = t * cos + pltpu.roll(t, HEAD_DIM // 2, axis=1) * sin
            o_ref[:, c * HEAD_DIM:(c + 1) * HEAD_DIM] = rot.astype(BF16)
    else:
        o_ref[...] = y.astype(BF16)


def _inproj(h, g, w, cos, sin, col0, width, rotary, tm=1024, tn=768):
    n, d = h.shape
    assert width % tn == 0 and col0 % tn == 0 and n % tm == 0
    jb = col0 // tn
    return pl.pallas_call(
        functools.partial(_inproj_kernel, rotary=rotary, tn=tn),
        grid=(n // tm, width // tn),
        in_specs=[
            pl.BlockSpec((tm, d), lambda i, j: (i, 0)),
            pl.BlockSpec((1, d), lambda i, j: (0, 0)),
            pl.BlockSpec((d, tn), lambda i, j: (0, jb + j)),
            pl.BlockSpec((tm, HEAD_DIM), lambda i, j: (i, 0)),
            pl.BlockSpec((tm, HEAD_DIM), lambda i, j: (i, 0)),
        ],
        out_specs=pl.BlockSpec((tm, tn), lambda i, j: (i, j)),
        out_shape=jax.ShapeDtypeStruct((n, width), BF16),
        scratch_shapes=[pltpu.VMEM((tm, d), BF16)],
        compiler_params=_params("parallel", "arbitrary"),
        name="norm_inproj_rotary" if rotary else "norm_inproj_plain",
    )(h, g, w, cos, sin)


def _attn_kernel(q_ref, k_ref, v_ref, o_ref, qf, kf, vf, acc, m_s, l_s, *, seq):
    qf[...] = q_ref[...].astype(F32)
    kf[...] = k_ref[...].astype(F32)
    vf[...] = v_ref[...].astype(F32)
    blk = KEYS_BACK
    row = lax.broadcasted_iota(jnp.int32, (blk, blk), 0)
    col = lax.broadcasted_iota(jnp.int32, (blk, blk), 1)
    mask_cur = col <= row
    mask_prev = col >= row
    scale = HEAD_DIM ** -0.5 * math.log2(math.e)
    last = len(DILATIONS) - 1
    nt = (((1,), (1,)), ((), ()))
    u = ATTN_INTERLEAVE

    for gi, (window, d) in enumerate(DILATIONS):
        assert window // d == blk
        n_blk = seq // d // blk
        nb = min(ATTN_TILE_BLOCKS, n_blk)
        tiles_per_class = n_blk // nb

        def rows(start, n, d=d):
            return pl.ds(start, n, stride=d) if d > 1 else pl.ds(start, n)

        def tile(start, first, gi=gi, d=d, nb=nb, rows=rows):
            ko = 0 if first else 1
            tq = nb * blk
            q_idx = rows(start, tq)
            kv_idx = rows(start - ko * blk * d, tq + ko * blk)
            q = qf[q_idx, :].astype(BF16)
            k = kf[kv_idx, :].astype(BF16)
            v = vf[kv_idx, :].astype(BF16)
            s = lax.dot_general(q, k, nt, preferred_element_type=F32)
            if gi > 0:
                m_prev = m_s[q_idx, :]
                l_prev = l_s[q_idx, :]
            zero = jnp.zeros((blk, blk), BF16)
            p_rows, m_rows, l_rows, alpha_rows = [], [], [], []
            for b in range(nb):
                rs = slice(b * blk, (b + 1) * blk)
                own = b + ko
                s_c = jnp.where(mask_cur, s[rs, own * blk:(own + 1) * blk] * scale, NEG_INF)
                s_top = s_c
                if own > 0:
                    s_p = jnp.where(mask_prev, s[rs, (own - 1) * blk:own * blk] * scale, NEG_INF)
                    s_top = jnp.maximum(s_c, s_p)
                m_blk = jnp.max(s_top, axis=-1, keepdims=True)
                if gi == 0:
                    m_new = jnp.broadcast_to(m_blk, (blk, blk))
                else:
                    m_old = m_prev[rs, :]
                    m_new = jnp.maximum(m_old, m_blk)
                p_c = jnp.exp2(s_c - m_new)
                p_all = p_c
                band = [zero] * (nb + ko)
                band[own] = p_c.astype(BF16)
                if own > 0:
                    p_p = jnp.exp2(s_p - m_new)
                    p_all = p_c + p_p
                    band[own - 1] = p_p.astype(BF16)
                p_rows.append(jnp.concatenate(band, axis=1))
                l_new = jnp.broadcast_to(jnp.sum(p_all, axis=-1, keepdims=True), (blk, blk))
                if gi > 0:
                    alpha = jnp.exp2(m_old - m_new)
                    l_new = alpha * l_prev[rs, :] + l_new
                    alpha_rows.append(alpha)
                m_rows.append(m_new)
                l_rows.append(l_new)
            pv = jnp.dot(jnp.concatenate(p_rows, axis=0), v, preferred_element_type=F32)
            l_all = jnp.concatenate(l_rows, axis=0)
            if gi > 0:
                pv = jnp.concatenate(alpha_rows, axis=0) * acc[q_idx, :] + pv
            if gi == last:
                acc[q_idx, :] = pv / l_all
            else:
                acc[q_idx, :] = pv
                m_s[q_idx, :] = jnp.concatenate(m_rows, axis=0)
                l_s[q_idx, :] = l_all

        span = d * blk * nb
        if tiles_per_class == 1:
            assert d % u == 0

            def body(it, c, tile=tile):
                for k in range(u):
                    tile(it * u + k, True)
                return c

            lax.fori_loop(0, d // u, body, 0)
        elif tiles_per_class == u:
            def body(r, c, tile=tile, span=span):
                tile(r, True)
                for k in range(1, u):
                    tile(r + k * span, False)
                return c

            if d == 1:
                body(0, 0)
            else:
                lax.fori_loop(0, d, body, 0)
        else:
            assert d == 1 and tiles_per_class % u == 0
            tile(0, True)
            for k in range(1, u):
                tile(k * span, False)

            def body(it, c, tile=tile, span=span):
                for k in range(u):
                    tile(pl.multiple_of((it * u + k) * span, blk), False)
                return c

            lax.fori_loop(1, tiles_per_class // u, body, 0)

    o_ref[...] = acc[...].astype(BF16)


def _attention(qk, rest, batch, seq):
    blockspec = lambda off: pl.BlockSpec((None, seq, HEAD_DIM), lambda b, h: (b, 0, off + h))
    return pl.pallas_call(
        functools.partial(_attn_kernel, seq=seq),
        grid=(batch, N_HEADS),
        in_specs=[blockspec(0), blockspec(N_HEADS), blockspec(0)],
        out_specs=blockspec(0),
        out_shape=jax.ShapeDtypeStruct((batch, seq, ATTN_WIDTH), BF16),
        scratch_shapes=[pltpu.VMEM((seq, HEAD_DIM), F32)] * 6,
        compiler_params=_params("parallel", "parallel"),
        name="dilated_attention",
    )(qk, qk, rest)


def _outproj_kernel(h_ref, a_ref, gb_ref, gc_ref, hv_ref, gch_ref, hvh_ref, cw_ref, wo_ref, o_ref,
                    ext, *, tm, seq):
    i = pl.program_id(0)
    first = (i * tm) % seq == 0
    u = gc_ref[...].astype(F32) * hv_ref[...].astype(F32)
    u_halo = gch_ref[...].astype(F32) * hvh_ref[...].astype(F32)
    ext[0:HALO, :] = jnp.where(first, 0.0, u_halo)
    ext[HALO:HALO + tm, :] = u
    w = cw_ref[...]
    conv = w[CONV_K - 1:CONV_K, :] * u
    for back in range(1, CONV_K):
        conv = conv + w[CONV_K - 1 - back:CONV_K - back, :] * ext[pl.ds(HALO - back, tm), :]
    c = (gb_ref[...].astype(F32) * conv).astype(BF16)
    y = jnp.dot(a_ref[...], wo_ref[0:ATTN_WIDTH, :], preferred_element_type=F32)
    y = y + jnp.dot(c, wo_ref[ATTN_WIDTH:, :], preferred_element_type=F32)
    o_ref[...] = h_ref[...] + y


def _outproj(h, attn, proj, conv_w, w_out, seq, tm=512):
    n, d = h.shape
    cb = ATTN_WIDTH // CONV_WIDTH
    hb = tm // HALO
    cur = lambda k: pl.BlockSpec((tm, CONV_WIDTH), lambda i: (i, cb + k))
    halo = lambda k: pl.BlockSpec((HALO, CONV_WIDTH), lambda i: (jnp.maximum(i * hb - 1, 0), cb + k))
    return pl.pallas_call(
        functools.partial(_outproj_kernel, tm=tm, seq=seq),
        grid=(n // tm,),
        in_specs=[
            pl.BlockSpec((tm, d), lambda i: (i, 0)),
            pl.BlockSpec((tm, ATTN_WIDTH), lambda i: (i, 0)),
            cur(0), cur(1), cur(2), halo(1), halo(2),
            pl.BlockSpec((CONV_K, CONV_WIDTH), lambda i: (0, 0)),
            pl.BlockSpec((d, d), lambda i: (0, 0)),
        ],
        out_specs=pl.BlockSpec((tm, d), lambda i: (i, 0)),
        out_shape=jax.ShapeDtypeStruct((n, d), F32),
        scratch_shapes=[pltpu.VMEM((HALO + tm, CONV_WIDTH), F32)],
        compiler_params=_params("parallel"),
        name="conv_outproj_residual",
    )(h, attn, proj, proj, proj, proj, proj, conv_w, w_out)


def _ffn_kernel(h_ref, g_ref, wg_ref, wu_ref, wd_ref, o_ref, xn_ref):
    j = pl.program_id(1)

    @pl.when(j == 0)
    def _():
        h = h_ref[...]
        xn_ref[...] = _rms(h, g_ref[...]).astype(BF16)
        o_ref[...] = h

    x = xn_ref[...]
    gate = jnp.dot(x, wg_ref[...], preferred_element_type=F32)
    up = jnp.dot(x, wu_ref[...], preferred_element_type=F32)
    act = (gate * jax.nn.sigmoid(gate) * up).astype(BF16)
    o_ref[...] += jnp.dot(act, wd_ref[...], preferred_element_type=F32)


def _ffn(h, g, wg, wu, wd, tm=1024, tf=256):
    n, d = h.shape
    f = wg.shape[1]
    assert n % tm == 0 and f % tf == 0
    return pl.pallas_call(
        _ffn_kernel,
        grid=(n // tm, f // tf),
        in_specs=[
            pl.BlockSpec((tm, d), lambda i, j: (i, 0)),
            pl.BlockSpec((1, d), lambda i, j: (0, 0)),
            pl.BlockSpec((d, tf), lambda i, j: (0, j)),
            pl.BlockSpec((d, tf), lambda i, j: (0, j)),
            pl.BlockSpec((tf, d), lambda i, j: (j, 0)),
        ],
        out_specs=pl.BlockSpec((tm, d), lambda i, j: (i, 0)),
        out_shape=jax.ShapeDtypeStruct((n, d), F32),
        scratch_shapes=[pltpu.VMEM((tm, d), BF16)],
        compiler_params=_params("parallel", "arbitrary"),
        name="norm_swiglu_residual",
    )(h, g, wg, wu, wd)


ROUTE_E1, ROUTE_E2, ROUTE_R1, ROUTE_R2, ROUTE_G1, ROUTE_G2 = range(6)


def _pool_router_kernel(h_ref, hh_ref, gm_ref, pw_ref, ps_ref, gf_ref, rw_ref,
                        hn_ref, xn_ref, route_ref, cnt_ref, ext, run, *, tm, seq):
    i = pl.program_id(0)

    @pl.when(i == 0)
    def _():
        run[...] = jnp.zeros_like(run)

    pos0 = (i * tm) % seq
    h = h_ref[...]
    gm = gm_ref[...]
    xn = _rms(h, gm)
    ext[0:HALO, :] = jnp.where(pos0 == 0, 0.0, _rms(hh_ref[...], gm))
    ext[HALO:HALO + tm, :] = xn
    t1 = (pos0 + 1 + lax.broadcasted_iota(jnp.int32, (tm, 1), 0)).astype(F32)
    group = h.shape[1] // len(POOL_WINDOWS)
    outs = []
    for gi, w in enumerate(POOL_WINDOWS):
        cols = slice(gi * group, (gi + 1) * group)
        x_g = xn[:, cols]
        s = x_g
        for back in range(1, w):
            s = s + ext[pl.ds(HALO - back, tm), cols]
        mixed = s / jnp.minimum(t1, float(w)) - x_g
        outs.append(jnp.dot(mixed.astype(BF16), pw_ref[gi], preferred_element_type=F32))
    hn = h + jnp.concatenate(outs, axis=1) * ps_ref[...]
    hn_ref[...] = hn

    xn2 = _rms(hn, gf_ref[...])
    xn_ref[...] = _pack_bf16_pairs(xn2)
    logits = jnp.dot(xn2, rw_ref[...], preferred_element_type=F32, precision=lax.Precision.HIGHEST)
    lane = lax.broadcasted_iota(jnp.int32, (tm, LANES), 1)
    logits = jnp.where(lane < N_EXPERTS, logits, -jnp.inf)
    m1 = jnp.max(logits, axis=-1, keepdims=True)
    e1 = jnp.min(jnp.where(logits == m1, lane, LANES), axis=-1, keepdims=True)
    rest = jnp.where(lane == e1, -jnp.inf, logits)
    m2 = jnp.max(rest, axis=-1, keepdims=True)
    e2 = jnp.min(jnp.where(rest == m2, lane, LANES), axis=-1, keepdims=True)
    ex = jnp.exp(m2 - m1)
    g1 = 1.0 / (1.0 + ex)
    g2 = ex / (1.0 + ex)

    hit1 = lane == e1
    hit2 = lane == e2
    onehot = jnp.where(hit1 | hit2, 1.0, 0.0)
    tri = (lax.broadcasted_iota(jnp.int32, (tm, tm), 1) < lax.broadcasted_iota(jnp.int32, (tm, tm), 0))
    before = jnp.dot(tri.astype(BF16), onehot.astype(BF16), preferred_element_type=F32) + run[...]
    r1 = jnp.sum(jnp.where(hit1, before, 0.0), axis=-1, keepdims=True)
    r2 = jnp.sum(jnp.where(hit2, before, 0.0), axis=-1, keepdims=True)
    run[...] = run[...] + jnp.sum(onehot, axis=0, keepdims=True)
    cnt_ref[...] = run[...]

    route = jnp.zeros((tm, LANES), F32)
    for slot, val in ((ROUTE_E1, e1.astype(F32)), (ROUTE_E2, e2.astype(F32)), (ROUTE_R1, r1),
                      (ROUTE_R2, r2), (ROUTE_G1, g1), (ROUTE_G2, g2)):
        route = jnp.where(lane == slot, val, route)
    route_ref[...] = route


def _pool_router(h, g_mix, pool_w, pool_scale, g_ffn, router_w, seq, tm=512):
    n, d = h.shape
    hb = tm // HALO
    row = pl.BlockSpec((tm, d), lambda i: (i, 0))
    vec = pl.BlockSpec((1, d), lambda i: (0, 0))
    rw = jnp.zeros((d, LANES), F32).at[:, :N_EXPERTS].set(router_w)
    return pl.pallas_call(
        functools.partial(_pool_router_kernel, tm=tm, seq=seq),
        grid=(n // tm,),
        in_specs=[
            row,
            pl.BlockSpec((HALO, d), lambda i: (jnp.maximum(i * hb - 1, 0), 0)),
            vec,
            pl.BlockSpec(pool_w.shape, lambda i: (0, 0, 0)),
            vec, vec,
            pl.BlockSpec((d, LANES), lambda i: (0, 0)),
        ],
        out_specs=[row, pl.BlockSpec((tm, d // 2), lambda i: (i, 0)),
                   pl.BlockSpec((tm, LANES), lambda i: (i, 0)), pl.BlockSpec((1, LANES), lambda i: (0, 0))],
        out_shape=[jax.ShapeDtypeStruct((n, d), F32), jax.ShapeDtypeStruct((n, d // 2), jnp.uint32),
                   jax.ShapeDtypeStruct((n, LANES), F32), jax.ShapeDtypeStruct((1, LANES), F32)],
        scratch_shapes=[pltpu.VMEM((HALO + tm, d), F32), pltpu.VMEM((1, LANES), F32)],
        compiler_params=_params("arbitrary"),
        name="pool_residual_router",
    )(h, h, g_mix, pool_w, pool_scale, g_ffn, rw)


def _row_copy(src_ref, src_row, dst_ref, dst_row, sem):
    return pltpu.make_async_copy(src_ref.at[pl.ds(src_row, 1)], dst_ref.at[pl.ds(dst_row, 1)], sem)


def _dispatch_kernel(d1_ref, d2_ref, x_ref, xs_in_ref, xs_ref, sem, *, tm):
    del xs_in_ref

    def issue(t, c):
        _row_copy(x_ref, t, xs_ref, d1_ref[0, t], sem).start()
        _row_copy(x_ref, t, xs_ref, d2_ref[0, t], sem).start()
        return c

    lax.fori_loop(0, tm, issue, 0, unroll=ROW_DMA_UNROLL)

    def drain(t, c):
        _row_copy(x_ref, 0, xs_ref, 0, sem).wait()
        _row_copy(x_ref, 0, xs_ref, 0, sem).wait()
        return c

    lax.fori_loop(0, tm, drain, 0, unroll=ROW_DMA_UNROLL)


def _dispatch(xn, dest1, dest2, n_rows, tm=512):
    n, w = xn.shape
    idx = pl.BlockSpec((None, 1, tm), lambda i: (i, 0, 0), memory_space=pltpu.SMEM)
    return pl.pallas_call(
        functools.partial(_dispatch_kernel, tm=tm),
        grid=(n // tm,),
        in_specs=[idx, idx, pl.BlockSpec((tm, w), lambda i: (i, 0)), pl.BlockSpec(memory_space=pl.ANY)],
        out_specs=pl.BlockSpec(memory_space=pl.ANY),
        out_shape=jax.ShapeDtypeStruct((n_rows, w), xn.dtype),
        scratch_shapes=[pltpu.SemaphoreType.DMA(())],
        input_output_aliases={3: 0},
        compiler_params=_params("arbitrary"),
        name="expert_dispatch",
    )(dest1.reshape(n // tm, 1, tm), dest2.reshape(n // tm, 1, tm), xn, jnp.zeros((n_rows, w), xn.dtype))


def _moe_kernel(te_ref, tr_ref, x_ref, wg_ref, wu_ref, wd_ref, o_ref, xb_ref):
    del te_ref
    i = pl.program_id(0)
    j = pl.program_id(1)
    rows = tr_ref[i]
    half = x_ref.shape[1]

    @pl.when(j == 0)
    def _():
        o_ref[...] = jnp.zeros_like(o_ref)

    @pl.when((rows > 0) & (j == 0))
    def _():
        lo, hi = _unpack_bf16_pairs(x_ref[...])
        xb_ref[:, :half] = lo
        xb_ref[:, half:] = hi

    @pl.when(rows > 0)
    def _():
        x = xb_ref[...]
        gate = jnp.dot(x, wg_ref[...].astype(BF16), preferred_element_type=F32)
        up = jnp.dot(x, wu_ref[...].astype(BF16), preferred_element_type=F32)
        act = (gate * jax.nn.sigmoid(gate) * up).astype(BF16)
        o_ref[...] += jnp.dot(act, wd_ref[...].astype(BF16), preferred_element_type=F32)


def _moe(xs, tile_expert, tile_rows, layer, wg, wu, wd, tm, tf=256):
    p, half = xs.shape
    d = 2 * half
    f = wg.shape[3]
    nj = f // tf
    assert p % tm == 0 and f % tf == 0

    def col(i, j, tr):
        return jnp.where(tr[i] > 0, j, nj - 1)

    grid_spec = pltpu.PrefetchScalarGridSpec(
        num_scalar_prefetch=2,
        grid=(p // tm, nj),
        in_specs=[
            pl.BlockSpec((tm, half), lambda i, j, te, tr: (jnp.where(tr[i] > 0, i, 0), 0)),
            pl.BlockSpec((None, None, d, tf), lambda i, j, te, tr: (layer, te[i], 0, col(i, j, tr))),
            pl.BlockSpec((None, None, d, tf), lambda i, j, te, tr: (layer, te[i], 0, col(i, j, tr))),
            pl.BlockSpec((None, None, tf, d), lambda i, j, te, tr: (layer, te[i], col(i, j, tr), 0)),
        ],
        out_specs=pl.BlockSpec((tm, d), lambda i, j, te, tr: (i, 0)),
        scratch_shapes=[pltpu.VMEM((tm, d), BF16)],
    )
    return pl.pallas_call(
        _moe_kernel,
        grid_spec=grid_spec,
        out_shape=jax.ShapeDtypeStruct((p, d), F32),
        compiler_params=_params("arbitrary", "arbitrary"),
        name="grouped_expert_swiglu",
    )(tile_expert, tile_rows, xs, wg, wu, wd)


def _combine_kernel(d1_ref, d2_ref, h_ref, route_ref, gn_ref, ys_ref, o_ref, b1, b2, sem, *, tm, final_norm):
    def issue(t, c):
        _row_copy(ys_ref, d1_ref[0, t], b1, t, sem).start()
        _row_copy(ys_ref, d2_ref[0, t], b2, t, sem).start()
        return c

    lax.fori_loop(0, tm, issue, 0, unroll=ROW_DMA_UNROLL)

    def drain(t, c):
        _row_copy(ys_ref, 0, b1, 0, sem).wait()
        _row_copy(ys_ref, 0, b2, 0, sem).wait()
        return c

    lax.fori_loop(0, tm, drain, 0, unroll=ROW_DMA_UNROLL)

    route = route_ref[...]
    g1 = route[:, ROUTE_G1:ROUTE_G1 + 1]
    g2 = route[:, ROUTE_G2:ROUTE_G2 + 1]
    out = h_ref[...] + (g1 * b1[...] + g2 * b2[...])
    if final_norm:
        out = _rms(out, gn_ref[...])
    o_ref[...] = out


def _combine(h, route, dest1, dest2, ys, g_final, final_norm, tm=512):
    n, d = h.shape
    idx = pl.BlockSpec((None, 1, tm), lambda i: (i, 0, 0), memory_space=pltpu.SMEM)
    row = pl.BlockSpec((tm, d), lambda i: (i, 0))
    return pl.pallas_call(
        functools.partial(_combine_kernel, tm=tm, final_norm=final_norm),
        grid=(n // tm,),
        in_specs=[idx, idx, row, pl.BlockSpec((tm, LANES), lambda i: (i, 0)),
                  pl.BlockSpec((1, d), lambda i: (0, 0)), pl.BlockSpec(memory_space=pl.ANY)],
        out_specs=row,
        out_shape=jax.ShapeDtypeStruct((n, d), F32),
        scratch_shapes=[pltpu.VMEM((tm, d), F32), pltpu.VMEM((tm, d), F32), pltpu.SemaphoreType.DMA(())],
        compiler_params=_params("arbitrary"),
        name="expert_combine_residual",
    )(dest1.reshape(n // tm, 1, tm), dest2.reshape(n // tm, 1, tm), h, route, g_final, ys)


def _moe_layer(h, route, counts, xn, layer, wg, wu, wd, g_final, final_norm, tm=1024):
    n, d = h.shape
    n_tiles_max = 2 * n // tm + N_EXPERTS
    counts = counts[0, :N_EXPERTS].astype(jnp.int32)
    padded = (counts + tm - 1) // tm * tm
    ends = jnp.cumsum(padded)
    offs = ends - padded
    col = lambda k: route[:, k].astype(jnp.int32)
    dest1 = offs[col(ROUTE_E1)] + col(ROUTE_R1)
    dest2 = offs[col(ROUTE_E2)] + col(ROUTE_R2)
    tile_start = jnp.arange(n_tiles_max, dtype=jnp.int32) * tm
    last_start = jnp.minimum(tile_start, ends[-1] - tm)
    tile_expert = jnp.sum((last_start[:, None] >= ends[None, :]).astype(jnp.int32), axis=1)
    tile_rows = jnp.clip((offs + counts)[tile_expert] - tile_start, 0, tm).astype(jnp.int32)
    xs = _dispatch(xn, dest1, dest2, n_tiles_max * tm)
    ys = _moe(xs, tile_expert, tile_rows, layer, wg, wu, wd, tm)
    return _combine(h, route, dest1, dest2, ys, g_final, final_norm)


def kernel(x, positions, norm_mix_even, w_in, conv_w, w_out, norm_ffn_even, w_gate, w_up, w_down,
           norm_mix_odd, pool_w, pool_scale, norm_ffn_odd, router_w, exp_w_gate, exp_w_up, exp_w_down,
           final_norm):
    batch, seq, d = x.shape
    n = batch * seq
    depth = norm_mix_even.shape[0] + norm_mix_odd.shape[0]
    assert depth % 2 == 0, "the final norm is fused into the last (odd) layer's combine"
    h = x.reshape(n, d)
    cos, sin = _rope_tables(positions)
    g_final = final_norm[None, :]
    for layer in range(depth):
        i = layer // 2
        if layer % 2 == 0:
            g_mix, w_in_b = norm_mix_even[i][None, :], w_in[i].astype(BF16)
            n_rot = 2 * ATTN_WIDTH
            qk = _inproj(h, g_mix, w_in_b, cos, sin, 0, n_rot, True)
            rest = _inproj(h, g_mix, w_in_b, cos, sin, n_rot, w_in_b.shape[1] - n_rot, False)
            attn = _attention(qk.reshape(batch, seq, -1), rest.reshape(batch, seq, -1), batch, seq)
            h = _outproj(h, attn.reshape(n, ATTN_WIDTH), rest, conv_w[i], w_out[i].astype(BF16), seq)
            h = _ffn(h, norm_ffn_even[i][None, :], w_gate[i].astype(BF16), w_up[i].astype(BF16),
                     w_down[i].astype(BF16))
        else:
            h, xn, route, counts = _pool_router(h, norm_mix_odd[i][None, :], pool_w[i].astype(BF16),
                                                pool_scale[i][None, :], norm_ffn_odd[i][None, :],
                                                router_w[i], seq)
            h = _moe_layer(h, route, counts, xn, i, exp_w_gate, exp_w_up, exp_w_down, g_final,
                           layer == depth - 1)
    return h.reshape(batch, seq, d)
```

```python
import functools
import math

import jax
import jax.numpy as jnp
from jax import lax
from jax.experimental import pallas as pl
from jax.experimental.pallas import tpu as pltpu

F32 = jnp.float32
BF16 = jnp.bfloat16

HEAD_DIM = 128
N_HEADS = 12
ATTN_WIDTH = N_HEADS * HEAD_DIM
CONV_WIDTH = 512
CONV_K = 3
DILATIONS = ((128, 1), (512, 4), (2048, 16))
KEYS_BACK = 128
ROPE_THETA = 10000.0
POOL_WINDOWS = (2, 4, 8, 16)
N_EXPERTS = 8
RMS_EPS = 1e-5
NEG_INF = -1e30
LANES = 128
HALO = 16
VMEM_LIMIT = 56 * 1024 * 1024
ATTN_TILE_BLOCKS = 4
ATTN_INTERLEAVE = 8
ROW_DMA_UNROLL = 8

def _params(*sem):
    return pltpu.CompilerParams(dimension_semantics=sem, vmem_limit_bytes=VMEM_LIMIT)


def _rms(x, g):
    return x * lax.rsqrt(jnp.mean(x * x, axis=-1, keepdims=True) + RMS_EPS) * g


def _pack_bf16_pairs(x):
    half = x.shape[1] // 2
    bits = lax.bitcast_convert_type(x.astype(BF16).astype(F32), jnp.uint32)
    return (bits[:, :half] >> 16) | (bits[:, half:] & jnp.uint32(0xFFFF0000))


def _unpack_bf16_pairs(words):
    lo = lax.bitcast_convert_type(words << 16, F32).astype(BF16)
    hi = lax.bitcast_convert_type(words & jnp.uint32(0xFFFF0000), F32).astype(BF16)
    return lo, hi


def _rope_kernel(pos_ref, freq_ref, sign_ref, cos_ref, sin_ref):
    ang = pos_ref[...].astype(F32) * freq_ref[...]
    cos_ref[...] = jnp.cos(ang)
    sin_ref[...] = jnp.sin(ang) * sign_ref[...]


def _rope_tables(positions, tm=1024):
    n = positions.size
    half = HEAD_DIM // 2
    inv_freq = jnp.power(ROPE_THETA, -jnp.arange(half, dtype=F32) / half)
    freq = jnp.concatenate([inv_freq, inv_freq])[None, :]
    sign = jnp.concatenate([-jnp.ones((half,), F32), jnp.ones((half,), F32)])[None, :]
    row = pl.BlockSpec((tm, HEAD_DIM), lambda i: (i, 0))
    const = pl.BlockSpec((1, HEAD_DIM), lambda i: (0, 0))
    return pl.pallas_call(
        _rope_kernel,
        grid=(n // tm,),
        in_specs=[pl.BlockSpec((tm, 1), lambda i: (i, 0)), const, const],
        out_specs=[row, row],
        out_shape=[jax.ShapeDtypeStruct((n, HEAD_DIM), F32)] * 2,
        compiler_params=_params("parallel"),
        name="rope_tables",
    )(positions.reshape(n, 1), freq, sign)


def _inproj_kernel(h_ref, g_ref, w_ref, cos_ref, sin_ref, o_ref, xn_ref, *, rotary, tn):
    j = pl.program_id(1)

    @pl.when(j == 0)
    def _():
        xn_ref[...] = _rms(h_ref[...], g_ref[...]).astype(BF16)

    y = jnp.dot(xn_ref[...], w_ref[...], preferred_element_type=F32)
    if rotary:
        cos = cos_ref[...]
        sin = sin_ref[...]
        for c in range(tn // HEAD_DIM):
            t = y[:, c * HEAD_DIM:(c + 1) * HEAD_DIM]
            rot = t * cos + pltpu.roll(t, HEAD_DIM // 2, axis=1) * sin
            o_ref[:, c * HEAD_DIM:(c + 1) * HEAD_DIM] = rot.astype(BF16)
    else:
        o_ref[...] = y.astype(BF16)


def _inproj(h, g, w, cos, sin, col0, width, rotary, tm=1024, tn=768):
    n, d = h.shape
    assert width % tn == 0 and col0 % tn == 0 and n % tm == 0
    jb = col0 // tn
    return pl.pallas_call(
        functools.partial(_inproj_kernel, rotary=rotary, tn=tn),
        grid=(n // tm, width // tn),
        in_specs=[
            pl.BlockSpec((tm, d), lambda i, j: (i, 0)),
            pl.BlockSpec((1, d), lambda i, j: (0, 0)),
            pl.BlockSpec((d, tn), lambda i, j: (0, jb + j)),
            pl.BlockSpec((tm, HEAD_DIM), lambda i, j: (i, 0)),
            pl.BlockSpec((tm, HEAD_DIM), lambda i, j: (i, 0)),
        ],
        out_specs=pl.BlockSpec((tm, tn), lambda i, j: (i, j)),
        out_shape=jax.ShapeDtypeStruct((n, width), BF16),
        scratch_shapes=[pltpu.VMEM((tm, d), BF16)],
        compiler_params=_params("parallel", "arbitrary"),
        name="norm_inproj_rotary" if rotary else "norm_inproj_plain",
    )(h, g, w, cos, sin)


def _attn_kernel(q_ref, k_ref, v_ref, o_ref, qf, kf, vf, acc, m_s, l_s, *, seq):
    qf[...] = q_ref[...].astype(F32)
    kf[...] = k_ref[...].astype(F32)
    vf[...] = v_ref[...].astype(F32)
    blk = KEYS_BACK
    row = lax.broadcasted_iota(jnp.int32, (blk, blk), 0)
    col = lax.broadcasted_iota(jnp.int32, (blk, blk), 1)
    mask_cur = col <= row
    mask_prev = col >= row
    mask_both = jnp.concatenate([mask_prev, mask_cur], axis=1)
    scale = HEAD_DIM ** -0.5 * math.log2(math.e)
    last = len(DILATIONS) - 1
    nt = (((1,), (1,)), ((), ()))
    u = ATTN_INTERLEAVE

    for gi, (window, d) in enumerate(DILATIONS):
        assert window // d == blk
        n_blk = seq // d // blk
        nb = min(ATTN_TILE_BLOCKS, n_blk)
        tiles_per_class = n_blk // nb

        def rows(start, n, d=d):
            return pl.ds(start, n, stride=d) if d > 1 else pl.ds(start, n)

        def load_tile(start, first, gi=gi, d=d, nb=nb, rows=rows):
            ko = 0 if first else 1
            q_idx = rows(start, nb * blk)
            kv_idx = rows(start - ko * blk * d, (nb + ko) * blk)
            t = dict(ko=ko, q_idx=q_idx, q=qf[q_idx, :].astype(BF16), k=kf[kv_idx, :].astype(BF16),
                     v=vf[kv_idx, :].astype(BF16))
            if gi > 0:
                t.update(m=m_s[q_idx, :], l=l_s[q_idx, :], a=acc[q_idx, :])
            return t

        def compute_tile(t, gi=gi, nb=nb):
            pv_rows, m_rows, l_rows = [], [], []
            for b in range(nb):
                rs = slice(b * blk, (b + 1) * blk)
                own = b + t["ko"]
                keys = slice(max(own - 1, 0) * blk, (own + 1) * blk)
                mask = mask_both if own > 0 else mask_cur
                s = lax.dot_general(t["q"][rs, :], t["k"][keys, :], nt, preferred_element_type=F32)
                s = jnp.where(mask, s * scale, NEG_INF)
                m_blk = jnp.max(s, axis=-1, keepdims=True)
                if gi == 0:
                    m_new = jnp.broadcast_to(m_blk, (blk, blk))
                else:
                    m_old = t["m"][rs, :]
                    m_new = jnp.maximum(m_old, m_blk)
                p_parts = [jnp.exp2(s[:, c * blk:(c + 1) * blk] - m_new) for c in range(s.shape[1] // blk)]
                p_sum = p_parts[0] if len(p_parts) == 1 else p_parts[0] + p_parts[1]
                l_new = jnp.broadcast_to(jnp.sum(p_sum, axis=-1, keepdims=True), (blk, blk))
                p = jnp.concatenate([x.astype(BF16) for x in p_parts], axis=1)
                pv = jnp.dot(p, t["v"][keys, :], preferred_element_type=F32)
                if gi > 0:
                    alpha = jnp.exp2(m_old - m_new)
                    l_new = alpha * t["l"][rs, :] + l_new
                    pv = alpha * t["a"][rs, :] + pv
                if gi == last:
                    pv = pv / l_new
                pv_rows.append(pv)
                m_rows.append(m_new)
                l_rows.append(l_new)
            return [jnp.concatenate(x, axis=0) for x in (pv_rows, m_rows, l_rows)]

        def run_tiles(tiles, gi=gi, load_tile=load_tile, compute_tile=compute_tile):
            loaded = [load_tile(start, first) for start, first in tiles]
            results = [compute_tile(t) for t in loaded]
            for t, (pv, m_new, l_new) in zip(loaded, results):
                acc[t["q_idx"], :] = pv
                if gi < last:
                    m_s[t["q_idx"], :] = m_new
                    l_s[t["q_idx"], :] = l_new

        span = d * blk * nb
        ug = min(u, d * tiles_per_class)
        if tiles_per_class <= ug:
            classes = ug // tiles_per_class
            assert ug % tiles_per_class == 0 and d % classes == 0

            def body(it, c, run_tiles=run_tiles, span=span, classes=classes, tpc=tiles_per_class):
                run_tiles([(it * classes + cl + k * span, k == 0) for cl in range(classes) for k in range(tpc)])
                return c

            if d == classes:
                body(0, 0)
            else:
                lax.fori_loop(0, d // classes, body, 0)
        else:
            assert d == 1 and tiles_per_class % ug == 0
            run_tiles([(k * span, k == 0) for k in range(ug)])

            def body(it, c, run_tiles=run_tiles, span=span, ug=ug):
                run_tiles([(pl.multiple_of((it * ug + k) * span, blk), False) for k in range(ug)])
                return c

            lax.fori_loop(1, tiles_per_class // ug, body, 0)

    o_ref[...] = acc[...].astype(BF16)


def _attention(qk, rest, batch, seq):
    blockspec = lambda off: pl.BlockSpec((None, seq, HEAD_DIM), lambda b, h: (b, 0, off + h))
    return pl.pallas_call(
        functools.partial(_attn_kernel, seq=seq),
        grid=(batch, N_HEADS),
        in_specs=[blockspec(0), blockspec(N_HEADS), blockspec(0)],
        out_specs=blockspec(0),
        out_shape=jax.ShapeDtypeStruct((batch, seq, ATTN_WIDTH), BF16),
        scratch_shapes=[pltpu.VMEM((seq, HEAD_DIM), F32)] * 6,
        compiler_params=_params("parallel", "parallel"),
        name="dilated_attention",
    )(qk, qk, rest)


def _outproj_kernel(h_ref, a_ref, gb_ref, gc_ref, hv_ref, gch_ref, hvh_ref, cw_ref, wo_ref, o_ref,
                    ext, *, tm, seq):
    i = pl.program_id(0)
    first = (i * tm) % seq == 0
    u = gc_ref[...].astype(F32) * hv_ref[...].astype(F32)
    u_halo = gch_ref[...].astype(F32) * hvh_ref[...].astype(F32)
    ext[0:HALO, :] = jnp.where(first, 0.0, u_halo)
    ext[HALO:HALO + tm, :] = u
    w = cw_ref[...]
    conv = w[CONV_K - 1:CONV_K, :] * u
    for back in range(1, CONV_K):
        conv = conv + w[CONV_K - 1 - back:CONV_K - back, :] * ext[pl.ds(HALO - back, tm), :]
    c = (gb_ref[...].astype(F32) * conv).astype(BF16)
    y = jnp.dot(a_ref[...], wo_ref[0:ATTN_WIDTH, :], preferred_element_type=F32)
    y = y + jnp.dot(c, wo_ref[ATTN_WIDTH:, :], preferred_element_type=F32)
    o_ref[...] = h_ref[...] + y


def _outproj(h, attn, proj, conv_w, w_out, seq, tm=512):
    n, d = h.shape
    cb = ATTN_WIDTH // CONV_WIDTH
    hb = tm // HALO
    cur = lambda k: pl.BlockSpec((tm, CONV_WIDTH), lambda i: (i, cb + k))
    halo = lambda k: pl.BlockSpec((HALO, CONV_WIDTH), lambda i: (jnp.maximum(i * hb - 1, 0), cb + k))
    return pl.pallas_call(
        functools.partial(_outproj_kernel, tm=tm, seq=seq),
        grid=(n // tm,),
        in_specs=[
            pl.BlockSpec((tm, d), lambda i: (i, 0)),
            pl.BlockSpec((tm, ATTN_WIDTH), lambda i: (i, 0)),
            cur(0), cur(1), cur(2), halo(1), halo(2),
            pl.BlockSpec((CONV_K, CONV_WIDTH), lambda i: (0, 0)),
            pl.BlockSpec((d, d), lambda i: (0, 0)),
        ],
        out_specs=pl.BlockSpec((tm, d), lambda i: (i, 0)),
        out_shape=jax.ShapeDtypeStruct((n, d), F32),
        scratch_shapes=[pltpu.VMEM((HALO + tm, CONV_WIDTH), F32)],
        compiler_params=_params("parallel"),
        name="conv_outproj_residual",
    )(h, attn, proj, proj, proj, proj, proj, conv_w, w_out)


def _ffn_kernel(h_ref, g_ref, wg_ref, wu_ref, wd_ref, o_ref, xn_ref):
    j = pl.program_id(1)

    @pl.when(j == 0)
    def _():
        h = h_ref[...]
        xn_ref[...] = _rms(h, g_ref[...]).astype(BF16)
        o_ref[...] = h

    x = xn_ref[...]
    gate = jnp.dot(x, wg_ref[...], preferred_element_type=F32)
    up = jnp.dot(x, wu_ref[...], preferred_element_type=F32)
    act = (gate * jax.nn.sigmoid(gate) * up).astype(BF16)
    o_ref[...] += jnp.dot(act, wd_ref[...], preferred_element_type=F32)


def _ffn(h, g, wg, wu, wd, tm=1024, tf=256):
    n, d = h.shape
    f = wg.shape[1]
    assert n % tm == 0 and f % tf == 0
    return pl.pallas_call(
        _ffn_kernel,
        grid=(n // tm, f // tf),
        in_specs=[
            pl.BlockSpec((tm, d), lambda i, j: (i, 0)),
            pl.BlockSpec((1, d), lambda i, j: (0, 0)),
            pl.BlockSpec((d, tf), lambda i, j: (0, j)),
            pl.BlockSpec((d, tf), lambda i, j: (0, j)),
            pl.BlockSpec((tf, d), lambda i, j: (j, 0)),
        ],
        out_specs=pl.BlockSpec((tm, d), lambda i, j: (i, 0)),
        out_shape=jax.ShapeDtypeStruct((n, d), F32),
        scratch_shapes=[pltpu.VMEM((tm, d), BF16)],
        compiler_params=_params("parallel", "arbitrary"),
        name="norm_swiglu_residual",
    )(h, g, wg, wu, wd)


ROUTE_E1, ROUTE_E2, ROUTE_R1, ROUTE_R2, ROUTE_G1, ROUTE_G2 = range(6)


def _pool_router_kernel(h_ref, hh_ref, gm_ref, pw_ref, ps_ref, gf_ref, rw_ref,
                        hn_ref, xn_ref, route_ref, cnt_ref, ext, run, *, tm, seq):
    i = pl.program_id(0)

    @pl.when(i == 0)
    def _():
        run[...] = jnp.zeros_like(run)

    pos0 = (i * tm) % seq
    h = h_ref[...]
    gm = gm_ref[...]
    xn = _rms(h, gm)
    ext[0:HALO, :] = jnp.where(pos0 == 0, 0.0, _rms(hh_ref[...], gm))
    ext[HALO:HALO + tm, :] = xn
    t1 = (pos0 + 1 + lax.broadcasted_iota(jnp.int32, (tm, 1), 0)).astype(F32)
    group = h.shape[1] // len(POOL_WINDOWS)
    outs = []
    for gi, w in enumerate(POOL_WINDOWS):
        cols = slice(gi * group, (gi + 1) * group)
        x_g = xn[:, cols]
        assert w & (w - 1) == 0 and w <= HALO
        s = ext[:, cols]
        shift = 1
        while shift < w:
            s = s + pltpu.roll(s, shift, axis=0)
            shift *= 2
        s = s[HALO:, :]
        mixed = s / jnp.minimum(t1, float(w)) - x_g
        outs.append(jnp.dot(mixed.astype(BF16), pw_ref[gi], preferred_element_type=F32))
    hn = h + jnp.concatenate(outs, axis=1) * ps_ref[...]
    hn_ref[...] = hn

    xn2 = _rms(hn, gf_ref[...])
    xn_ref[...] = _pack_bf16_pairs(xn2)
    logits = jnp.dot(xn2, rw_ref[...], preferred_element_type=F32, precision=lax.Precision.HIGHEST)
    lane = lax.broadcasted_iota(jnp.int32, (tm, LANES), 1)
    logits = jnp.where(lane < N_EXPERTS, logits, -jnp.inf)
    m1 = jnp.max(logits, axis=-1, keepdims=True)
    e1 = jnp.min(jnp.where(logits == m1, lane, LANES), axis=-1, keepdims=True)
    rest = jnp.where(lane == e1, -jnp.inf, logits)
    m2 = jnp.max(rest, axis=-1, keepdims=True)
    e2 = jnp.min(jnp.where(rest == m2, lane, LANES), axis=-1, keepdims=True)
    ex = jnp.exp(m2 - m1)
    g1 = 1.0 / (1.0 + ex)
    g2 = ex / (1.0 + ex)

    hit1 = lane == e1
    hit2 = lane == e2
    onehot = jnp.where(hit1 | hit2, 1.0, 0.0)
    tri = (lax.broadcasted_iota(jnp.int32, (tm, tm), 1) < lax.broadcasted_iota(jnp.int32, (tm, tm), 0))
    before = jnp.dot(tri.astype(BF16), onehot.astype(BF16), preferred_element_type=F32) + run[...]
    r1 = jnp.sum(jnp.where(hit1, before, 0.0), axis=-1, keepdims=True)
    r2 = jnp.sum(jnp.where(hit2, before, 0.0), axis=-1, keepdims=True)
    run[...] = run[...] + jnp.sum(onehot, axis=0, keepdims=True)
    cnt_ref[...] = run[...]

    route = jnp.zeros((tm, LANES), F32)
    for slot, val in ((ROUTE_E1, e1.astype(F32)), (ROUTE_E2, e2.astype(F32)), (ROUTE_R1, r1),
                      (ROUTE_R2, r2), (ROUTE_G1, g1), (ROUTE_G2, g2)):
        route = jnp.where(lane == slot, val, route)
    route_ref[...] = route


def _pool_router(h, g_mix, pool_w, pool_scale, g_ffn, router_w, seq, tm=512):
    n, d = h.shape
    hb = tm // HALO
    row = pl.BlockSpec((tm, d), lambda i: (i, 0))
    vec = pl.BlockSpec((1, d), lambda i: (0, 0))
    rw = jnp.zeros((d, LANES), F32).at[:, :N_EXPERTS].set(router_w)
    return pl.pallas_call(
        functools.partial(_pool_router_kernel, tm=tm, seq=seq),
        grid=(n // tm,),
        in_specs=[
            row,
            pl.BlockSpec((HALO, d), lambda i: (jnp.maximum(i * hb - 1, 0), 0)),
            vec,
            pl.BlockSpec(pool_w.shape, lambda i: (0, 0, 0)),
            vec, vec,
            pl.BlockSpec((d, LANES), lambda i: (0, 0)),
        ],
        out_specs=[row, pl.BlockSpec((tm, d // 2), lambda i: (i, 0)),
                   pl.BlockSpec((tm, LANES), lambda i: (i, 0)), pl.BlockSpec((1, LANES), lambda i: (0, 0))],
        out_shape=[jax.ShapeDtypeStruct((n, d), F32), jax.ShapeDtypeStruct((n, d // 2), jnp.uint32),
                   jax.ShapeDtypeStruct((n, LANES), F32), jax.ShapeDtypeStruct((1, LANES), F32)],
        scratch_shapes=[pltpu.VMEM((HALO + tm, d), F32), pltpu.VMEM((1, LANES), F32)],
        compiler_params=_params("arbitrary"),
        name="pool_residual_router",
    )(h, h, g_mix, pool_w, pool_scale, g_ffn, rw)


def _row_copy(src_ref, src_row, dst_ref, dst_row, sem):
    return pltpu.make_async_copy(src_ref.at[pl.ds(src_row, 1)], dst_ref.at[pl.ds(dst_row, 1)], sem)


def _dispatch_kernel(d1_ref, d2_ref, x_ref, xs_in_ref, xs_ref, sem, *, tm):
    del xs_in_ref

    def issue(t, c):
        _row_copy(x_ref, t, xs_ref, d1_ref[0, t], sem).start()
        _row_copy(x_ref, t, xs_ref, d2_ref[0, t], sem).start()
        return c

    lax.fori_loop(0, tm, issue, 0, unroll=ROW_DMA_UNROLL)

    def drain(t, c):
        _row_copy(x_ref, 0, xs_ref, 0, sem).wait()
        _row_copy(x_ref, 0, xs_ref, 0, sem).wait()
        return c

    lax.fori_loop(0, tm, drain, 0, unroll=ROW_DMA_UNROLL)


def _dispatch(xn, dest1, dest2, n_rows, tm=512):
    n, w = xn.shape
    idx = pl.BlockSpec((None, 1, tm), lambda i: (i, 0, 0), memory_space=pltpu.SMEM)
    return pl.pallas_call(
        functools.partial(_dispatch_kernel, tm=tm),
        grid=(n // tm,),
        in_specs=[idx, idx, pl.BlockSpec((tm, w), lambda i: (i, 0)), pl.BlockSpec(memory_space=pl.ANY)],
        out_specs=pl.BlockSpec(memory_space=pl.ANY),
        out_shape=jax.ShapeDtypeStruct((n_rows, w), xn.dtype),
        scratch_shapes=[pltpu.SemaphoreType.DMA(())],
        input_output_aliases={3: 0},
        compiler_params=_params("arbitrary"),
        name="expert_dispatch",
    )(dest1.reshape(n // tm, 1, tm), dest2.reshape(n // tm, 1, tm), xn, jnp.zeros((n_rows, w), xn.dtype))


def _moe_kernel(te_ref, tr_ref, x_ref, wg_ref, wu_ref, wd_ref, o_ref, xb_ref):
    del te_ref
    i = pl.program_id(0)
    j = pl.program_id(1)
    rows = tr_ref[i]
    half = x_ref.shape[1]

    @pl.when(j == 0)
    def _():
        o_ref[...] = jnp.zeros_like(o_ref)

    @pl.when((rows > 0) & (j == 0))
    def _():
        lo, hi = _unpack_bf16_pairs(x_ref[...])
        xb_ref[:, :half] = lo
        xb_ref[:, half:] = hi

    @pl.when(rows > 0)
    def _():
        x = xb_ref[...]
        gate = jnp.dot(x, wg_ref[...].astype(BF16), preferred_element_type=F32)
        up = jnp.dot(x, wu_ref[...].astype(BF16), preferred_element_type=F32)
        act = (gate * jax.nn.sigmoid(gate) * up).astype(BF16)
        o_ref[...] += jnp.dot(act, wd_ref[...].astype(BF16), preferred_element_type=F32)


def _moe(xs, tile_expert, tile_rows, layer, wg, wu, wd, tm, tf=256):
    p, half = xs.shape
    d = 2 * half
    f = wg.shape[3]
    nj = f // tf
    assert p % tm == 0 and f % tf == 0

    def col(i, j, tr):
        return jnp.where(tr[i] > 0, j, nj - 1)

    grid_spec = pltpu.PrefetchScalarGridSpec(
        num_scalar_prefetch=2,
        grid=(p // tm, nj),
        in_specs=[
            pl.BlockSpec((tm, half), lambda i, j, te, tr: (jnp.where(tr[i] > 0, i, 0), 0)),
            pl.BlockSpec((None, None, d, tf), lambda i, j, te, tr: (layer, te[i], 0, col(i, j, tr))),
            pl.BlockSpec((None, None, d, tf), lambda i, j, te, tr: (layer, te[i], 0, col(i, j, tr))),
            pl.BlockSpec((None, None, tf, d), lambda i, j, te, tr: (layer, te[i], col(i, j, tr), 0)),
        ],
        out_specs=pl.BlockSpec((tm, d), lambda i, j, te, tr: (i, 0)),
        scratch_shapes=[pltpu.VMEM((tm, d), BF16)],
    )
    return pl.pallas_call(
        _moe_kernel,
        grid_spec=grid_spec,
        out_shape=jax.ShapeDtypeStruct((p, d), F32),
        compiler_params=_params("arbitrary", "arbitrary"),
        name="grouped_expert_swiglu",
    )(tile_expert, tile_rows, xs, wg, wu, wd)


def _combine_kernel(d1_ref, d2_ref, h_ref, route_ref, gn_ref, ys_ref, o_ref, b1, b2, sem, *, tm, final_norm):
    def issue(t, c):
        _row_copy(ys_ref, d1_ref[0, t], b1, t, sem).start()
        _row_copy(ys_ref, d2_ref[0, t], b2, t, sem).start()
        return c

    lax.fori_loop(0, tm, issue, 0, unroll=ROW_DMA_UNROLL)

    def drain(t, c):
        _row_copy(ys_ref, 0, b1, 0, sem).wait()
        _row_copy(ys_ref, 0, b2, 0, sem).wait()
        return c

    lax.fori_loop(0, tm, drain, 0, unroll=ROW_DMA_UNROLL)

    route = route_ref[...]
    g1 = route[:, ROUTE_G1:ROUTE_G1 + 1]
    g2 = route[:, ROUTE_G2:ROUTE_G2 + 1]
    out = h_ref[...] + (g1 * b1[...] + g2 * b2[...])
    if final_norm:
        out = _rms(out, gn_ref[...])
    o_ref[...] = out


def _combine(h, route, dest1, dest2, ys, g_final, final_norm, tm=512):
    n, d = h.shape
    idx = pl.BlockSpec((None, 1, tm), lambda i: (i, 0, 0), memory_space=pltpu.SMEM)
    row = pl.BlockSpec((tm, d), lambda i: (i, 0))
    return pl.pallas_call(
        functools.partial(_combine_kernel, tm=tm, final_norm=final_norm),
        grid=(n // tm,),
        in_specs=[idx, idx, row, pl.BlockSpec((tm, LANES), lambda i: (i, 0)),
                  pl.BlockSpec((1, d), lambda i: (0, 0)), pl.BlockSpec(memory_space=pl.ANY)],
        out_specs=row,
        out_shape=jax.ShapeDtypeStruct((n, d), F32),
        scratch_shapes=[pltpu.VMEM((tm, d), F32), pltpu.VMEM((tm, d), F32), pltpu.SemaphoreType.DMA(())],
        compiler_params=_params("arbitrary"),
        name="expert_combine_residual",
    )(dest1.reshape(n // tm, 1, tm), dest2.reshape(n // tm, 1, tm), h, route, g_final, ys)


def _moe_layer(h, route, counts, xn, layer, wg, wu, wd, g_final, final_norm, tm=1024):
    n, d = h.shape
    n_tiles_max = 2 * n // tm + N_EXPERTS
    counts = counts[0, :N_EXPERTS].astype(jnp.int32)
    padded = (counts + tm - 1) // tm * tm
    ends = jnp.cumsum(padded)
    offs = ends - padded
    col = lambda k: route[:, k].astype(jnp.int32)
    dest1 = offs[col(ROUTE_E1)] + col(ROUTE_R1)
    dest2 = offs[col(ROUTE_E2)] + col(ROUTE_R2)
    tile_start = jnp.arange(n_tiles_max, dtype=jnp.int32) * tm
    last_start = jnp.minimum(tile_start, ends[-1] - tm)
    tile_expert = jnp.sum((last_start[:, None] >= ends[None, :]).astype(jnp.int32), axis=1)
    tile_rows = jnp.clip((offs + counts)[tile_expert] - tile_start, 0, tm).astype(jnp.int32)
    xs = _dispatch(xn, dest1, dest2, n_tiles_max * tm)
    ys = _moe(xs, tile_expert, tile_rows, layer, wg, wu, wd, tm)
    return _combine(h, route, dest1, dest2, ys, g_final, final_norm)


def kernel(x, positions, norm_mix_even, w_in, conv_w, w_out, norm_ffn_even, w_gate, w_up, w_down,
           norm_mix_odd, pool_w, pool_scale, norm_ffn_odd, router_w, exp_w_gate, exp_w_up, exp_w_down,
           final_norm):
    batch, seq, d = x.shape
    n = batch * seq
    depth = norm_mix_even.shape[0] + norm_mix_odd.shape[0]
    assert depth % 2 == 0, "the final norm is fused into the last (odd) layer's combine"
    h = x.reshape(n, d)
    cos, sin = _rope_tables(positions)
    g_final = final_norm[None, :]
    for layer in range(depth):
        i = layer // 2
        if layer % 2 == 0:
            g_mix, w_in_b = norm_mix_even[i][None, :], w_in[i].astype(BF16)
            n_rot = 2 * ATTN_WIDTH
            qk = _inproj(h, g_mix, w_in_b, cos, sin, 0, n_rot, True)
            rest = _inproj(h, g_mix, w_in_b, cos, sin, n_rot, w_in_b.shape[1] - n_rot, False)
            attn = _attention(qk.reshape(batch, seq, -1), rest.reshape(batch, seq, -1), batch, seq)
            h = _outproj(h, attn.reshape(n, ATTN_WIDTH), rest, conv_w[i], w_out[i].astype(BF16), seq)
            h = _ffn(h, norm_ffn_even[i][None, :], w_gate[i].astype(BF16), w_up[i].astype(BF16),
                     w_down[i].astype(BF16))
        else:
            h, xn, route, counts = _pool_router(h, norm_mix_odd[i][None, :], pool_w[i].astype(BF16),
                                                pool_scale[i][None, :], norm_ffn_odd[i][None, :],
                                                router_w[i], seq)
            h = _moe_layer(h, route, counts, xn, i, exp_w_gate, exp_w_up, exp_w_down, g_final,
                           layer == depth - 1)
    return h.reshape(batch, seq, d)
```

```python
import functools
import math

import jax
import jax.numpy as jnp
from jax import lax
from jax.experimental import pallas as pl
from jax.experimental.pallas import tpu as pltpu

F32 = jnp.float32
BF16 = jnp.bfloat16

HEAD_DIM = 128
N_HEADS = 12
ATTN_WIDTH = N_HEADS * HEAD_DIM
CONV_WIDTH = 512
CONV_K = 3
DILATIONS = ((128, 1), (512, 4), (2048, 16))
KEYS_BACK = 128
ROPE_THETA = 10000.0
POOL_WINDOWS = (2, 4, 8, 16)
N_EXPERTS = 8
RMS_EPS = 1e-5
NEG_INF = -1e30
LANES = 128
HALO = 16
VMEM_LIMIT = 56 * 1024 * 1024
ATTN_TILE_BLOCKS = 4
ATTN_BASE_DILATION = 4
ATTN_INTERLEAVE = 16
ROW_DMA_UNROLL = 8

def _params(*sem):
    return pltpu.CompilerParams(dimension_semantics=sem, vmem_limit_bytes=VMEM_LIMIT)


def _rms(x, g):
    return x * lax.rsqrt(jnp.mean(x * x, axis=-1, keepdims=True) + RMS_EPS) * g


def _pack_bf16_pairs(x):
    half = x.shape[1] // 2
    bits = lax.bitcast_convert_type(x.astype(BF16).astype(F32), jnp.uint32)
    return (bits[:, :half] >> 16) | (bits[:, half:] & jnp.uint32(0xFFFF0000))


def _unpack_bf16_pairs(words):
    lo = lax.bitcast_convert_type(words << 16, F32).astype(BF16)
    hi = lax.bitcast_convert_type(words & jnp.uint32(0xFFFF0000), F32).astype(BF16)
    return lo, hi


def _rope_kernel(pos_ref, freq_ref, sign_ref, cos_ref, sin_ref):
    ang = pos_ref[...].astype(F32) * freq_ref[...]
    cos_ref[...] = jnp.cos(ang)
    sin_ref[...] = jnp.sin(ang) * sign_ref[...]


def _rope_tables(positions, tm=1024):
    n = positions.size
    half = HEAD_DIM // 2
    inv_freq = jnp.power(ROPE_THETA, -jnp.arange(half, dtype=F32) / half)
    freq = jnp.concatenate([inv_freq, inv_freq])[None, :]
    sign = jnp.concatenate([-jnp.ones((half,), F32), jnp.ones((half,), F32)])[None, :]
    row = pl.BlockSpec((tm, HEAD_DIM), lambda i: (i, 0))
    const = pl.BlockSpec((1, HEAD_DIM), lambda i: (0, 0))
    return pl.pallas_call(
        _rope_kernel,
        grid=(n // tm,),
        in_specs=[pl.BlockSpec((tm, 1), lambda i: (i, 0)), const, const],
        out_specs=[row, row],
        out_shape=[jax.ShapeDtypeStruct((n, HEAD_DIM), F32)] * 2,
        compiler_params=_params("parallel"),
        name="rope_tables",
    )(positions.reshape(n, 1), freq, sign)


def _inproj_kernel(h_ref, g_ref, w_ref, cos_ref, sin_ref, o_ref, xn_ref, *, rotary, tn):
    j = pl.program_id(1)

    @pl.when(j == 0)
    def _():
        xn_ref[...] = _rms(h_ref[...], g_ref[...]).astype(BF16)

    y = jnp.dot(xn_ref[...], w_ref[...], preferred_element_type=F32)
    if rotary:
        cos = cos_ref[...]
        sin = sin_ref[...]
        for c in range(tn // HEAD_DIM):
            t = y[:, c * HEAD_DIM:(c + 1) * HEAD_DIM]
            rot = t * cos + pltpu.roll(t, HEAD_DIM // 2, axis=1) * sin
            o_ref[:, c * HEAD_DIM:(c + 1) * HEAD_DIM] = rot.astype(BF16)
    else:
        o_ref[...] = y.astype(BF16)


def _inproj(h, g, w, cos, sin, col0, width, rotary, tm=1024, tn=768):
    n, d = h.shape
    assert width % tn == 0 and col0 % tn == 0 and n % tm == 0
    jb = col0 // tn
    return pl.pallas_call(
        functools.partial(_inproj_kernel, rotary=rotary, tn=tn),
        grid=(n // tm, width // tn),
        in_specs=[
            pl.BlockSpec((tm, d), lambda i, j: (i, 0)),
            pl.BlockSpec((1, d), lambda i, j: (0, 0)),
            pl.BlockSpec((d, tn), lambda i, j: (0, jb + j)),
            pl.BlockSpec((tm, HEAD_DIM), lambda i, j: (i, 0)),
            pl.BlockSpec((tm, HEAD_DIM), lambda i, j: (i, 0)),
        ],
        out_specs=pl.BlockSpec((tm, tn), lambda i, j: (i, j)),
        out_shape=jax.ShapeDtypeStruct((n, width), BF16),
        scratch_shapes=[pltpu.VMEM((tm, d), BF16)],
        compiler_params=_params("parallel", "arbitrary"),
        name="norm_inproj_rotary" if rotary else "norm_inproj_plain",
    )(h, g, w, cos, sin)


def _attn_kernel(q_ref, k_ref, v_ref, o_ref, nat, q16, k16, v16, acc, m_s, l_s, *, seq):
    blk = KEYS_BACK
    n_cls = ATTN_BASE_DILATION
    cls_len = seq // n_cls
    scale = HEAD_DIM ** -0.5 * math.log2(math.e)
    last = len(DILATIONS) - 1
    nt = (((1,), (1,)), ((), ()))
    u = ATTN_INTERLEAVE

    def cls_rows(r):
        return pl.ds(pl.multiple_of(r * cls_len, cls_len), cls_len)

    for src_ref, dst in ((q_ref, q16), (k_ref, k16), (v_ref, v16)):
        nat[...] = src_ref[...].astype(F32)

        def split(r, c, dst=dst):
            dst[cls_rows(r), :] = nat[pl.ds(r, cls_len, stride=n_cls), :]
            return c

        lax.fori_loop(0, n_cls, split, 0)

    for gi, (window, d) in enumerate(DILATIONS):
        assert window // d == blk and (n_cls % d == 0 or d % n_cls == 0)
        n_blk = seq // d // blk
        nb = min(ATTN_TILE_BLOCKS, n_blk)
        tiles_per_class = n_blk // nb
        pieces = max(n_cls // d, 1)
        stride = max(d // n_cls, 1)
        plen = blk // pieces
        idx0 = lax.broadcasted_iota(jnp.int32, (blk, blk), 0)
        idx1 = lax.broadcasted_iota(jnp.int32, (blk, blk), 1)
        pos0 = pieces * (idx0 % plen) + idx0 // plen
        pos1 = pieces * (idx1 % plen) + idx1 // plen
        mask_cur = pos1 <= pos0
        mask_prev = pos1 >= pos0
        mask_both = jnp.concatenate([mask_prev, mask_cur], axis=1)

        def piece_rows(rc, b, c, d=d, plen=plen, stride=stride):
            if stride > 1:
                return pl.ds((rc % n_cls) * cls_len + rc // n_cls + stride * blk * b, blk, stride=stride)
            return pl.ds(pl.multiple_of((d * c + rc) * cls_len + plen * b, 8), plen)

        def load_block(ref, rc, b, pieces=pieces, piece_rows=piece_rows):
            return jnp.concatenate([ref[piece_rows(rc, b, c), :] for c in range(pieces)], axis=0)

        def store_block(ref, rc, b, val, pieces=pieces, plen=plen, piece_rows=piece_rows):
            for c in range(pieces):
                ref[piece_rows(rc, b, c), :] = val[c * plen:(c + 1) * plen, :]

        def load_tile(rc, b0, first, gi=gi, nb=nb, load_block=load_block):
            ko = 0 if first else 1
            t = dict(rc=rc, b0=b0, ko=ko,
                     q=[load_block(q16, rc, b0 + b).astype(BF16) for b in range(nb)],
                     k=[load_block(k16, rc, b0 + b - ko).astype(BF16) for b in range(nb + ko)],
                     v=[load_block(v16, rc, b0 + b - ko).astype(BF16) for b in range(nb + ko)])
            if gi > 0:
                t.update(m=[load_block(m_s, rc, b0 + b) for b in range(nb)],
                         l=[load_block(l_s, rc, b0 + b) for b in range(nb)],
                         a=[load_block(acc, rc, b0 + b) for b in range(nb)])
            return t

        def compute_tile(t, gi=gi, nb=nb, mask_cur=mask_cur, mask_both=mask_both):
            out = []
            for b in range(nb):
                own = b + t["ko"]
                if own > 0:
                    k = jnp.concatenate([t["k"][own - 1], t["k"][own]], axis=0)
                    v = jnp.concatenate([t["v"][own - 1], t["v"][own]], axis=0)
                    mask = mask_both
                else:
                    k, v, mask = t["k"][own], t["v"][own], mask_cur
                s = lax.dot_general(t["q"][b], k, nt, preferred_element_type=F32)
                s = jnp.where(mask, s * scale, NEG_INF)
                m_blk = jnp.max(s, axis=-1, keepdims=True)
                if gi == 0:
                    m_new = jnp.broadcast_to(m_blk, (blk, blk))
                else:
                    m_old = t["m"][b]
                    m_new = jnp.maximum(m_old, m_blk)
                p_parts = [jnp.exp2(s[:, c * blk:(c + 1) * blk] - m_new) for c in range(s.shape[1] // blk)]
                p_sum = p_parts[0] if len(p_parts) == 1 else p_parts[0] + p_parts[1]
                l_new = jnp.broadcast_to(jnp.sum(p_sum, axis=-1, keepdims=True), (blk, blk))
                p = jnp.concatenate([x.astype(BF16) for x in p_parts], axis=1)
                pv = jnp.dot(p, v, preferred_element_type=F32)
                if gi > 0:
                    alpha = jnp.exp2(m_old - m_new)
                    l_new = alpha * t["l"][b] + l_new
                    pv = alpha * t["a"][b] + pv
                if gi == last:
                    pv = pv / l_new
                out.append((pv, m_new, l_new))
            return out

        def run_tiles(tiles, gi=gi, nb=nb, load_tile=load_tile, compute_tile=compute_tile,
                      store_block=store_block):
            loaded = [load_tile(rc, b0, first) for rc, b0, first in tiles]
            results = [compute_tile(t) for t in loaded]
            for t, res in zip(loaded, results):
                for b, (pv, m_new, l_new) in enumerate(res):
                    store_block(acc, t["rc"], t["b0"] + b, pv)
                    if gi < last:
                        store_block(m_s, t["rc"], t["b0"] + b, m_new)
                        store_block(l_s, t["rc"], t["b0"] + b, l_new)

        ug = min(u, d * tiles_per_class)
        if tiles_per_class <= ug:
            classes = ug // tiles_per_class
            assert ug % tiles_per_class == 0 and d % classes == 0

            def body(it, c, run_tiles=run_tiles, nb=nb, classes=classes, tpc=tiles_per_class):
                run_tiles([(it * classes + cl, k * nb, k == 0) for cl in range(classes) for k in range(tpc)])
                return c

            if d == classes:
                body(0, 0)
            else:
                lax.fori_loop(0, d // classes, body, 0)
        else:
            assert d == 1 and tiles_per_class % ug == 0
            run_tiles([(0, k * nb, k == 0) for k in range(ug)])

            def body(it, c, run_tiles=run_tiles, nb=nb, ug=ug):
                run_tiles([(0, (it * ug + k) * nb, False) for k in range(ug)])
                return c

            lax.fori_loop(1, tiles_per_class // ug, body, 0)

    def merge(r, c):
        nat[pl.ds(r, cls_len, stride=n_cls), :] = acc[cls_rows(r), :]
        return c

    lax.fori_loop(0, n_cls, merge, 0)
    o_ref[...] = nat[...].astype(BF16)


def _attention(qk, rest, batch, seq):
    blockspec = lambda off: pl.BlockSpec((None, seq, HEAD_DIM), lambda b, h: (b, 0, off + h))
    return pl.pallas_call(
        functools.partial(_attn_kernel, seq=seq),
        grid=(batch, N_HEADS),
        in_specs=[blockspec(0), blockspec(N_HEADS), blockspec(0)],
        out_specs=blockspec(0),
        out_shape=jax.ShapeDtypeStruct((batch, seq, ATTN_WIDTH), BF16),
        scratch_shapes=[pltpu.VMEM((seq, HEAD_DIM), F32)] * 7,
        compiler_params=_params("parallel", "parallel"),
        name="dilated_attention",
    )(qk, qk, rest)


def _outproj_kernel(h_ref, a_ref, gb_ref, gc_ref, hv_ref, gch_ref, hvh_ref, cw_ref, wo_ref, o_ref,
                    ext, *, tm, seq):
    i = pl.program_id(0)
    first = (i * tm) % seq == 0
    u = gc_ref[...].astype(F32) * hv_ref[...].astype(F32)
    u_halo = gch_ref[...].astype(F32) * hvh_ref[...].astype(F32)
    ext[0:HALO, :] = jnp.where(first, 0.0, u_halo)
    ext[HALO:HALO + tm, :] = u
    w = cw_ref[...]
    conv = w[CONV_K - 1:CONV_K, :] * u
    for back in range(1, CONV_K):
        conv = conv + w[CONV_K - 1 - back:CONV_K - back, :] * ext[pl.ds(HALO - back, tm), :]
    c = (gb_ref[...].astype(F32) * conv).astype(BF16)
    y = jnp.dot(a_ref[...], wo_ref[0:ATTN_WIDTH, :], preferred_element_type=F32)
    y = y + jnp.dot(c, wo_ref[ATTN_WIDTH:, :], preferred_element_type=F32)
    o_ref[...] = h_ref[...] + y


def _outproj(h, attn, proj, conv_w, w_out, seq, tm=512):
    n, d = h.shape
    cb = ATTN_WIDTH // CONV_WIDTH
    hb = tm // HALO
    cur = lambda k: pl.BlockSpec((tm, CONV_WIDTH), lambda i: (i, cb + k))
    halo = lambda k: pl.BlockSpec((HALO, CONV_WIDTH), lambda i: (jnp.maximum(i * hb - 1, 0), cb + k))
    return pl.pallas_call(
        functools.partial(_outproj_kernel, tm=tm, seq=seq),
        grid=(n // tm,),
        in_specs=[
            pl.BlockSpec((tm, d), lambda i: (i, 0)),
            pl.BlockSpec((tm, ATTN_WIDTH), lambda i: (i, 0)),
            cur(0), cur(1), cur(2), halo(1), halo(2),
            pl.BlockSpec((CONV_K, CONV_WIDTH), lambda i: (0, 0)),
            pl.BlockSpec((d, d), lambda i: (0, 0)),
        ],
        out_specs=pl.BlockSpec((tm, d), lambda i: (i, 0)),
        out_shape=jax.ShapeDtypeStruct((n, d), F32),
        scratch_shapes=[pltpu.VMEM((HALO + tm, CONV_WIDTH), F32)],
        compiler_params=_params("parallel"),
        name="conv_outproj_residual",
    )(h, attn, proj, proj, proj, proj, proj, conv_w, w_out)


def _ffn_kernel(h_ref, g_ref, wg_ref, wu_ref, wd_ref, o_ref, xn_ref):
    j = pl.program_id(1)

    @pl.when(j == 0)
    def _():
        h = h_ref[...]
        xn_ref[...] = _rms(h, g_ref[...]).astype(BF16)
        o_ref[...] = h

    x = xn_ref[...]
    gate = jnp.dot(x, wg_ref[...], preferred_element_type=F32)
    up = jnp.dot(x, wu_ref[...], preferred_element_type=F32)
    act = (gate * jax.nn.sigmoid(gate) * up).astype(BF16)
    o_ref[...] += jnp.dot(act, wd_ref[...], preferred_element_type=F32)


def _ffn(h, g, wg, wu, wd, tm=1024, tf=256):
    n, d = h.shape
    f = wg.shape[1]
    assert n % tm == 0 and f % tf == 0
    return pl.pallas_call(
        _ffn_kernel,
        grid=(n // tm, f // tf),
        in_specs=[
            pl.BlockSpec((tm, d), lambda i, j: (i, 0)),
            pl.BlockSpec((1, d), lambda i, j: (0, 0)),
            pl.BlockSpec((d, tf), lambda i, j: (0, j)),
            pl.BlockSpec((d, tf), lambda i, j: (0, j)),
            pl.BlockSpec((tf, d), lambda i, j: (j, 0)),
        ],
        out_specs=pl.BlockSpec((tm, d), lambda i, j: (i, 0)),
        out_shape=jax.ShapeDtypeStruct((n, d), F32),
        scratch_shapes=[pltpu.VMEM((tm, d), BF16)],
        compiler_params=_params("parallel", "arbitrary"),
        name="norm_swiglu_residual",
    )(h, g, wg, wu, wd)


ROUTE_E1, ROUTE_E2, ROUTE_R1, ROUTE_R2, ROUTE_G1, ROUTE_G2 = range(6)


def _pool_router_kernel(h_ref, hh_ref, gm_ref, pw_ref, ps_ref, gf_ref, rw_ref,
                        hn_ref, xn_ref, route_ref, cnt_ref, ext, run, *, tm, seq):
    i = pl.program_id(0)

    @pl.when(i == 0)
    def _():
        run[...] = jnp.zeros_like(run)

    pos0 = (i * tm) % seq
    h = h_ref[...]
    gm = gm_ref[...]
    xn = _rms(h, gm)
    ext[0:HALO, :] = jnp.where(pos0 == 0, 0.0, _rms(hh_ref[...], gm))
    ext[HALO:HALO + tm, :] = xn
    t1 = (pos0 + 1 + lax.broadcasted_iota(jnp.int32, (tm, 1), 0)).astype(F32)
    group = h.shape[1] // len(POOL_WINDOWS)
    outs = []
    for gi, w in enumerate(POOL_WINDOWS):
        cols = slice(gi * group, (gi + 1) * group)
        x_g = xn[:, cols]
        assert w & (w - 1) == 0 and w <= HALO
        s = ext[:, cols]
        shift = 1
        while shift < w:
            s = s + pltpu.roll(s, shift, axis=0)
            shift *= 2
        s = s[HALO:, :]
        mixed = s / jnp.minimum(t1, float(w)) - x_g
        outs.append(jnp.dot(mixed.astype(BF16), pw_ref[gi], preferred_element_type=F32))
    hn = h + jnp.concatenate(outs, axis=1) * ps_ref[...]
    hn_ref[...] = hn

    xn2 = _rms(hn, gf_ref[...])
    xn_ref[...] = _pack_bf16_pairs(xn2)
    logits = jnp.dot(xn2, rw_ref[...], preferred_element_type=F32, precision=lax.Precision.HIGHEST)
    lane = lax.broadcasted_iota(jnp.int32, (tm, LANES), 1)
    logits = jnp.where(lane < N_EXPERTS, logits, -jnp.inf)
    m1 = jnp.max(logits, axis=-1, keepdims=True)
    e1 = jnp.min(jnp.where(logits == m1, lane, LANES), axis=-1, keepdims=True)
    rest = jnp.where(lane == e1, -jnp.inf, logits)
    m2 = jnp.max(rest, axis=-1, keepdims=True)
    e2 = jnp.min(jnp.where(rest == m2, lane, LANES), axis=-1, keepdims=True)
    ex = jnp.exp(m2 - m1)
    g1 = 1.0 / (1.0 + ex)
    g2 = ex / (1.0 + ex)

    hit1 = lane == e1
    hit2 = lane == e2
    onehot = jnp.where(hit1 | hit2, 1.0, 0.0)
    tri = (lax.broadcasted_iota(jnp.int32, (tm, tm), 1) < lax.broadcasted_iota(jnp.int32, (tm, tm), 0))
    before = jnp.dot(tri.astype(BF16), onehot.astype(BF16), preferred_element_type=F32) + run[...]
    r1 = jnp.sum(jnp.where(hit1, before, 0.0), axis=-1, keepdims=True)
    r2 = jnp.sum(jnp.where(hit2, before, 0.0), axis=-1, keepdims=True)
    run[...] = run[...] + jnp.sum(onehot, axis=0, keepdims=True)
    cnt_ref[...] = run[...]

    route = jnp.zeros((tm, LANES), F32)
    for slot, val in ((ROUTE_E1, e1.astype(F32)), (ROUTE_E2, e2.astype(F32)), (ROUTE_R1, r1),
                      (ROUTE_R2, r2), (ROUTE_G1, g1), (ROUTE_G2, g2)):
        route = jnp.where(lane == slot, val, route)
    route_ref[...] = route


def _pool_router(h, g_mix, pool_w, pool_scale, g_ffn, router_w, seq, tm=512):
    n, d = h.shape
    hb = tm // HALO
    row = pl.BlockSpec((tm, d), lambda i: (i, 0))
    vec = pl.BlockSpec((1, d), lambda i: (0, 0))
    rw = jnp.zeros((d, LANES), F32).at[:, :N_EXPERTS].set(router_w)
    return pl.pallas_call(
        functools.partial(_pool_router_kernel, tm=tm, seq=seq),
        grid=(n // tm,),
        in_specs=[
            row,
            pl.BlockSpec((HALO, d), lambda i: (jnp.maximum(i * hb - 1, 0), 0)),
            vec,
            pl.BlockSpec(pool_w.shape, lambda i: (0, 0, 0)),
            vec, vec,
            pl.BlockSpec((d, LANES), lambda i: (0, 0)),
        ],
        out_specs=[row, pl.BlockSpec((tm, d // 2), lambda i: (i, 0)),
                   pl.BlockSpec((tm, LANES), lambda i: (i, 0)), pl.BlockSpec((1, LANES), lambda i: (0, 0))],
        out_shape=[jax.ShapeDtypeStruct((n, d), F32), jax.ShapeDtypeStruct((n, d // 2), jnp.uint32),
                   jax.ShapeDtypeStruct((n, LANES), F32), jax.ShapeDtypeStruct((1, LANES), F32)],
        scratch_shapes=[pltpu.VMEM((HALO + tm, d), F32), pltpu.VMEM((1, LANES), F32)],
        compiler_params=_params("arbitrary"),
        name="pool_residual_router",
    )(h, h, g_mix, pool_w, pool_scale, g_ffn, rw)


def _row_copy(src_ref, src_row, dst_ref, dst_row, sem):
    return pltpu.make_async_copy(src_ref.at[pl.ds(src_row, 1)], dst_ref.at[pl.ds(dst_row, 1)], sem)


def _dispatch_kernel(d1_ref, d2_ref, x_ref, xs_in_ref, xs_ref, sem, *, tm):
    del xs_in_ref

    def issue(t, c):
        _row_copy(x_ref, t, xs_ref, d1_ref[0, t], sem).start()
        _row_copy(x_ref, t, xs_ref, d2_ref[0, t], sem).start()
        return c

    lax.fori_loop(0, tm, issue, 0, unroll=ROW_DMA_UNROLL)

    def drain(t, c):
        _row_copy(x_ref, 0, xs_ref, 0, sem).wait()
        _row_copy(x_ref, 0, xs_ref, 0, sem).wait()
        return c

    lax.fori_loop(0, tm, drain, 0, unroll=ROW_DMA_UNROLL)


def _dispatch(xn, dest1, dest2, n_rows, tm=512):
    n, w = xn.shape
    idx = pl.BlockSpec((None, 1, tm), lambda i: (i, 0, 0), memory_space=pltpu.SMEM)
    return pl.pallas_call(
        functools.partial(_dispatch_kernel, tm=tm),
        grid=(n // tm,),
        in_specs=[idx, idx, pl.BlockSpec((tm, w), lambda i: (i, 0)), pl.BlockSpec(memory_space=pl.ANY)],
        out_specs=pl.BlockSpec(memory_space=pl.ANY),
        out_shape=jax.ShapeDtypeStruct((n_rows, w), xn.dtype),
        scratch_shapes=[pltpu.SemaphoreType.DMA(())],
        input_output_aliases={3: 0},
        compiler_params=_params("arbitrary"),
        name="expert_dispatch",
    )(dest1.reshape(n // tm, 1, tm), dest2.reshape(n // tm, 1, tm), xn, jnp.zeros((n_rows, w), xn.dtype))


def _moe_kernel(te_ref, tr_ref, x_ref, wg_ref, wu_ref, wd_ref, o_ref, xb_ref):
    del te_ref
    i = pl.program_id(0)
    j = pl.program_id(1)
    rows = tr_ref[i]
    half = x_ref.shape[1]

    @pl.when(j == 0)
    def _():
        o_ref[...] = jnp.zeros_like(o_ref)

    @pl.when((rows > 0) & (j == 0))
    def _():
        lo, hi = _unpack_bf16_pairs(x_ref[...])
        xb_ref[:, :half] = lo
        xb_ref[:, half:] = hi

    def swiglu_top_rows(n_rows):
        r = pl.ds(0, n_rows)
        x = xb_ref[r, :]
        gate = jnp.dot(x, wg_ref[...].astype(BF16), preferred_element_type=F32)
        up = jnp.dot(x, wu_ref[...].astype(BF16), preferred_element_type=F32)
        act = (gate * jax.nn.sigmoid(gate) * up).astype(BF16)
        o_ref[r, :] += jnp.dot(act, wd_ref[...].astype(BF16), preferred_element_type=F32)

    tm = o_ref.shape[0]
    pl.when(rows > tm // 2)(lambda: swiglu_top_rows(tm))
    pl.when((rows > tm // 4) & (rows <= tm // 2))(lambda: swiglu_top_rows(tm // 2))
    pl.when((rows > 0) & (rows <= tm // 4))(lambda: swiglu_top_rows(tm // 4))


def _moe(xs, tile_expert, tile_rows, layer, wg, wu, wd, tm, tf=256):
    p, half = xs.shape
    d = 2 * half
    f = wg.shape[3]
    nj = f // tf
    assert p % tm == 0 and f % tf == 0

    def col(i, j, tr):
        return jnp.where(tr[i] > 0, j, nj - 1)

    grid_spec = pltpu.PrefetchScalarGridSpec(
        num_scalar_prefetch=2,
        grid=(p // tm, nj),
        in_specs=[
            pl.BlockSpec((tm, half), lambda i, j, te, tr: (jnp.where(tr[i] > 0, i, 0), 0)),
            pl.BlockSpec((None, None, d, tf), lambda i, j, te, tr: (layer, te[i], 0, col(i, j, tr))),
            pl.BlockSpec((None, None, d, tf), lambda i, j, te, tr: (layer, te[i], 0, col(i, j, tr))),
            pl.BlockSpec((None, None, tf, d), lambda i, j, te, tr: (layer, te[i], col(i, j, tr), 0)),
        ],
        out_specs=pl.BlockSpec((tm, d), lambda i, j, te, tr: (i, 0)),
        scratch_shapes=[pltpu.VMEM((tm, d), BF16)],
    )
    return pl.pallas_call(
        _moe_kernel,
        grid_spec=grid_spec,
        out_shape=jax.ShapeDtypeStruct((p, d), F32),
        compiler_params=_params("arbitrary", "arbitrary"),
        name="grouped_expert_swiglu",
    )(tile_expert, tile_rows, xs, wg, wu, wd)


def _combine_kernel(d1_ref, d2_ref, h_ref, route_ref, gn_ref, ys_ref, o_ref, b1, b2, sem, *, tm, final_norm):
    def issue(t, c):
        _row_copy(ys_ref, d1_ref[0, t], b1, t, sem).start()
        _row_copy(ys_ref, d2_ref[0, t], b2, t, sem).start()
        return c

    lax.fori_loop(0, tm, issue, 0, unroll=ROW_DMA_UNROLL)

    def drain(t, c):
        _row_copy(ys_ref, 0, b1, 0, sem).wait()
        _row_copy(ys_ref, 0, b2, 0, sem).wait()
        return c

    lax.fori_loop(0, tm, drain, 0, unroll=ROW_DMA_UNROLL)

    route = route_ref[...]
    g1 = route[:, ROUTE_G1:ROUTE_G1 + 1]
    g2 = route[:, ROUTE_G2:ROUTE_G2 + 1]
    out = h_ref[...] + (g1 * b1[...] + g2 * b2[...])
    if final_norm:
        out = _rms(out, gn_ref[...])
    o_ref[...] = out


def _combine(h, route, dest1, dest2, ys, g_final, final_norm, tm=512):
    n, d = h.shape
    idx = pl.BlockSpec((None, 1, tm), lambda i: (i, 0, 0), memory_space=pltpu.SMEM)
    row = pl.BlockSpec((tm, d), lambda i: (i, 0))
    return pl.pallas_call(
        functools.partial(_combine_kernel, tm=tm, final_norm=final_norm),
        grid=(n // tm,),
        in_specs=[idx, idx, row, pl.BlockSpec((tm, LANES), lambda i: (i, 0)),
                  pl.BlockSpec((1, d), lambda i: (0, 0)), pl.BlockSpec(memory_space=pl.ANY)],
        out_specs=row,
        out_shape=jax.ShapeDtypeStruct((n, d), F32),
        scratch_shapes=[pltpu.VMEM((tm, d), F32), pltpu.VMEM((tm, d), F32), pltpu.SemaphoreType.DMA(())],
        compiler_params=_params("arbitrary"),
        name="expert_combine_residual",
    )(dest1.reshape(n // tm, 1, tm), dest2.reshape(n // tm, 1, tm), h, route, g_final, ys)


def _moe_layer(h, route, counts, xn, layer, wg, wu, wd, g_final, final_norm, tm=1024):
    n, d = h.shape
    n_tiles_max = 2 * n // tm + N_EXPERTS
    counts = counts[0, :N_EXPERTS].astype(jnp.int32)
    padded = (counts + tm - 1) // tm * tm
    ends = jnp.cumsum(padded)
    offs = ends - padded
    col = lambda k: route[:, k].astype(jnp.int32)
    dest1 = offs[col(ROUTE_E1)] + col(ROUTE_R1)
    dest2 = offs[col(ROUTE_E2)] + col(ROUTE_R2)
    tile_start = jnp.arange(n_tiles_max, dtype=jnp.int32) * tm
    last_start = jnp.minimum(tile_start, ends[-1] - tm)
    tile_expert = jnp.sum((last_start[:, None] >= ends[None, :]).astype(jnp.int32), axis=1)
    tile_rows = jnp.clip((offs + counts)[tile_expert] - tile_start, 0, tm).astype(jnp.int32)
    xs = _dispatch(xn, dest1, dest2, n_tiles_max * tm)
    ys = _moe(xs, tile_expert, tile_rows, layer, wg, wu, wd, tm)
    return _combine(h, route, dest1, dest2, ys, g_final, final_norm)


def kernel(x, positions, norm_mix_even, w_in, conv_w, w_out, norm_ffn_even, w_gate, w_up, w_down,
           norm_mix_odd, pool_w, pool_scale, norm_ffn_odd, router_w, exp_w_gate, exp_w_up, exp_w_down,
           final_norm):
    batch, seq, d = x.shape
    n = batch * seq
    depth = norm_mix_even.shape[0] + norm_mix_odd.shape[0]
    assert depth % 2 == 0, "the final norm is fused into the last (odd) layer's combine"
    h = x.reshape(n, d)
    cos, sin = _rope_tables(positions)
    g_final = final_norm[None, :]
    for layer in range(depth):
        i = layer // 2
        if layer % 2 == 0:
            g_mix, w_in_b = norm_mix_even[i][None, :], w_in[i].astype(BF16)
            n_rot = 2 * ATTN_WIDTH
            qk = _inproj(h, g_mix, w_in_b, cos, sin, 0, n_rot, True)
            rest = _inproj(h, g_mix, w_in_b, cos, sin, n_rot, w_in_b.shape[1] - n_rot, False)
            attn = _attention(qk.reshape(batch, seq, -1), rest.reshape(batch, seq, -1), batch, seq)
            h = _outproj(h, attn.reshape(n, ATTN_WIDTH), rest, conv_w[i], w_out[i].astype(BF16), seq)
            h = _ffn(h, norm_ffn_even[i][None, :], w_gate[i].astype(BF16), w_up[i].astype(BF16),
                     w_down[i].astype(BF16))
        else:
            h, xn, route, counts = _pool_router(h, norm_mix_odd[i][None, :], pool_w[i].astype(BF16),
                                                pool_scale[i][None, :], norm_ffn_odd[i][None, :],
                                                router_w[i], seq)
            h = _moe_layer(h, route, counts, xn, i, exp_w_gate, exp_w_up, exp_w_down, g_final,
                           layer == depth - 1)
    return h.reshape(batch, seq, d)
```

```python
import functools
import math

import jax
import jax.numpy as jnp
from jax import lax
from jax.experimental import pallas as pl
from jax.experimental.pallas import tpu as pltpu

F32 = jnp.float32
BF16 = jnp.bfloat16

HEAD_DIM = 128
N_HEADS = 12
ATTN_WIDTH = N_HEADS * HEAD_DIM
CONV_WIDTH = 512
CONV_K = 3
DILATIONS = ((128, 1), (512, 4), (2048, 16))
KEYS_BACK = 128
ROPE_THETA = 10000.0
POOL_WINDOWS = (2, 4, 8, 16)
N_EXPERTS = 8
RMS_EPS = 1e-5
NEG_INF = -1e30
LANES = 128
HALO = 16
VMEM_LIMIT = 56 * 1024 * 1024
ATTN_TILE_BLOCKS = 4
ATTN_BASE_DILATION = 4
ATTN_INTERLEAVE = 16
ROW_DMA_UNROLL = 8

def _params(*sem):
    return pltpu.CompilerParams(dimension_semantics=sem, vmem_limit_bytes=VMEM_LIMIT)


def _rms(x, g):
    return x * lax.rsqrt(jnp.mean(x * x, axis=-1, keepdims=True) + RMS_EPS) * g


def _pack_bf16_pairs(x):
    half = x.shape[1] // 2
    bits = lax.bitcast_convert_type(x.astype(BF16).astype(F32), jnp.uint32)
    return (bits[:, :half] >> 16) | (bits[:, half:] & jnp.uint32(0xFFFF0000))


def _unpack_bf16_pairs(words):
    lo = lax.bitcast_convert_type(words << 16, F32).astype(BF16)
    hi = lax.bitcast_convert_type(words & jnp.uint32(0xFFFF0000), F32).astype(BF16)
    return lo, hi


def _rope_kernel(pos_ref, freq_ref, sign_ref, cos_ref, sin_ref):
    ang = pos_ref[...].astype(F32) * freq_ref[...]
    cos_ref[...] = jnp.cos(ang)
    sin_ref[...] = jnp.sin(ang) * sign_ref[...]


def _rope_tables(positions, tm=1024):
    n = positions.size
    half = HEAD_DIM // 2
    inv_freq = jnp.power(ROPE_THETA, -jnp.arange(half, dtype=F32) / half)
    freq = jnp.concatenate([inv_freq, inv_freq])[None, :]
    sign = jnp.concatenate([-jnp.ones((half,), F32), jnp.ones((half,), F32)])[None, :]
    row = pl.BlockSpec((tm, HEAD_DIM), lambda i: (i, 0))
    const = pl.BlockSpec((1, HEAD_DIM), lambda i: (0, 0))
    return pl.pallas_call(
        _rope_kernel,
        grid=(n // tm,),
        in_specs=[pl.BlockSpec((tm, 1), lambda i: (i, 0)), const, const],
        out_specs=[row, row],
        out_shape=[jax.ShapeDtypeStruct((n, HEAD_DIM), F32)] * 2,
        compiler_params=_params("parallel"),
        name="rope_tables",
    )(positions.reshape(n, 1), freq, sign)


def _inproj_kernel(h_ref, g_ref, w_ref, cos_ref, sin_ref, o_ref, xn_ref, *, rotary, tn):
    j = pl.program_id(1)

    @pl.when(j == 0)
    def _():
        xn_ref[...] = _rms(h_ref[...], g_ref[...]).astype(BF16)

    y = jnp.dot(xn_ref[...], w_ref[...], preferred_element_type=F32)
    if rotary:
        cos = cos_ref[...]
        sin = sin_ref[...]
        for c in range(tn // HEAD_DIM):
            t = y[:, c * HEAD_DIM:(c + 1) * HEAD_DIM]
            rot = t * cos + pltpu.roll(t, HEAD_DIM // 2, axis=1) * sin
            o_ref[:, c * HEAD_DIM:(c + 1) * HEAD_DIM] = rot.astype(BF16)
    else:
        o_ref[...] = y.astype(BF16)


def _inproj(h, g, w, cos, sin, col0, width, rotary, tm=1024, tn=1536):
    n, d = h.shape
    assert width % tn == 0 and col0 % tn == 0 and n % tm == 0
    jb = col0 // tn
    return pl.pallas_call(
        functools.partial(_inproj_kernel, rotary=rotary, tn=tn),
        grid=(n // tm, width // tn),
        in_specs=[
            pl.BlockSpec((tm, d), lambda i, j: (i, 0)),
            pl.BlockSpec((1, d), lambda i, j: (0, 0)),
            pl.BlockSpec((d, tn), lambda i, j: (0, jb + j)),
            pl.BlockSpec((tm, HEAD_DIM), lambda i, j: (i, 0)),
            pl.BlockSpec((tm, HEAD_DIM), lambda i, j: (i, 0)),
        ],
        out_specs=pl.BlockSpec((tm, tn), lambda i, j: (i, j)),
        out_shape=jax.ShapeDtypeStruct((n, width), BF16),
        scratch_shapes=[pltpu.VMEM((tm, d), BF16)],
        compiler_params=_params("parallel", "arbitrary"),
        name="norm_inproj_rotary" if rotary else "norm_inproj_plain",
    )(h, g, w, cos, sin)


def _attn_kernel(q_ref, k_ref, v_ref, o_ref, nat, q16, k16, v16, acc, m_s, l_s, *, seq):
    blk = KEYS_BACK
    n_cls = ATTN_BASE_DILATION
    cls_len = seq // n_cls
    scale = HEAD_DIM ** -0.5 * math.log2(math.e)
    last = len(DILATIONS) - 1
    nt = (((1,), (1,)), ((), ()))
    u = ATTN_INTERLEAVE

    def cls_rows(r):
        return pl.ds(pl.multiple_of(r * cls_len, cls_len), cls_len)

    for src_ref, dst in ((q_ref, q16), (k_ref, k16), (v_ref, v16)):
        nat[...] = src_ref[...].astype(F32)

        def split(r, c, dst=dst):
            dst[cls_rows(r), :] = nat[pl.ds(r, cls_len, stride=n_cls), :]
            return c

        lax.fori_loop(0, n_cls, split, 0)

    for gi, (window, d) in enumerate(DILATIONS):
        assert window // d == blk and (n_cls % d == 0 or d % n_cls == 0)
        n_blk = seq // d // blk
        nb = min(ATTN_TILE_BLOCKS, n_blk)
        tiles_per_class = n_blk // nb
        pieces = max(n_cls // d, 1)
        stride = max(d // n_cls, 1)
        plen = blk // pieces
        idx0 = lax.broadcasted_iota(jnp.int32, (blk, blk), 0)
        idx1 = lax.broadcasted_iota(jnp.int32, (blk, blk), 1)
        pos0 = pieces * (idx0 % plen) + idx0 // plen
        pos1 = pieces * (idx1 % plen) + idx1 // plen
        mask_cur = pos1 <= pos0
        mask_prev = pos1 >= pos0
        mask_both = jnp.concatenate([mask_prev, mask_cur], axis=1)

        def piece_rows(rc, b, c, d=d, plen=plen, stride=stride):
            if stride > 1:
                return pl.ds((rc % n_cls) * cls_len + rc // n_cls + stride * blk * b, blk, stride=stride)
            return pl.ds(pl.multiple_of((d * c + rc) * cls_len + plen * b, 8), plen)

        def load_block(ref, rc, b, pieces=pieces, piece_rows=piece_rows):
            return jnp.concatenate([ref[piece_rows(rc, b, c), :] for c in range(pieces)], axis=0)

        def store_block(ref, rc, b, val, pieces=pieces, plen=plen, piece_rows=piece_rows):
            for c in range(pieces):
                ref[piece_rows(rc, b, c), :] = val[c * plen:(c + 1) * plen, :]

        def load_tile(rc, b0, first, gi=gi, nb=nb, load_block=load_block):
            ko = 0 if first else 1
            t = dict(rc=rc, b0=b0, ko=ko,
                     q=[load_block(q16, rc, b0 + b).astype(BF16) for b in range(nb)],
                     k=[load_block(k16, rc, b0 + b - ko).astype(BF16) for b in range(nb + ko)],
                     v=[load_block(v16, rc, b0 + b - ko).astype(BF16) for b in range(nb + ko)])
            if gi > 0:
                t.update(m=[load_block(m_s, rc, b0 + b) for b in range(nb)],
                         l=[load_block(l_s, rc, b0 + b) for b in range(nb)],
                         a=[load_block(acc, rc, b0 + b) for b in range(nb)])
            return t

        def compute_tile(t, gi=gi, nb=nb, mask_cur=mask_cur, mask_both=mask_both):
            out = []
            for b in range(nb):
                own = b + t["ko"]
                if own > 0:
                    k = jnp.concatenate([t["k"][own - 1], t["k"][own]], axis=0)
                    v = jnp.concatenate([t["v"][own - 1], t["v"][own]], axis=0)
                    mask = mask_both
                else:
                    k, v, mask = t["k"][own], t["v"][own], mask_cur
                s = lax.dot_general(t["q"][b], k, nt, preferred_element_type=F32)
                s = jnp.where(mask, s * scale, NEG_INF)
                m_blk = jnp.max(s, axis=-1, keepdims=True)
                if gi == 0:
                    m_new = jnp.broadcast_to(m_blk, (blk, blk))
                else:
                    m_old = t["m"][b]
                    m_new = jnp.maximum(m_old, m_blk)
                p_parts = [jnp.exp2(s[:, c * blk:(c + 1) * blk] - m_new) for c in range(s.shape[1] // blk)]
                p_sum = p_parts[0] if len(p_parts) == 1 else p_parts[0] + p_parts[1]
                l_new = jnp.broadcast_to(jnp.sum(p_sum, axis=-1, keepdims=True), (blk, blk))
                p = jnp.concatenate([x.astype(BF16) for x in p_parts], axis=1)
                pv = jnp.dot(p, v, preferred_element_type=F32)
                if gi > 0:
                    alpha = jnp.exp2(m_old - m_new)
                    l_new = alpha * t["l"][b] + l_new
                    pv = alpha * t["a"][b] + pv
                if gi == last:
                    pv = pv / l_new
                out.append((pv, m_new, l_new))
            return out

        def run_tiles(tiles, gi=gi, nb=nb, load_tile=load_tile, compute_tile=compute_tile,
                      store_block=store_block):
            loaded = [load_tile(rc, b0, first) for rc, b0, first in tiles]
            results = [compute_tile(t) for t in loaded]
            for t, res in zip(loaded, results):
                for b, (pv, m_new, l_new) in enumerate(res):
                    store_block(acc, t["rc"], t["b0"] + b, pv)
                    if gi < last:
                        store_block(m_s, t["rc"], t["b0"] + b, m_new)
                        store_block(l_s, t["rc"], t["b0"] + b, l_new)

        ug = min(u, d * tiles_per_class)
        if tiles_per_class <= ug:
            classes = ug // tiles_per_class
            assert ug % tiles_per_class == 0 and d % classes == 0

            def body(it, c, run_tiles=run_tiles, nb=nb, classes=classes, tpc=tiles_per_class):
                run_tiles([(it * classes + cl, k * nb, k == 0) for cl in range(classes) for k in range(tpc)])
                return c

            if d == classes:
                body(0, 0)
            else:
                lax.fori_loop(0, d // classes, body, 0)
        else:
            assert d == 1 and tiles_per_class % ug == 0
            run_tiles([(0, k * nb, k == 0) for k in range(ug)])

            def body(it, c, run_tiles=run_tiles, nb=nb, ug=ug):
                run_tiles([(0, (it * ug + k) * nb, False) for k in range(ug)])
                return c

            lax.fori_loop(1, tiles_per_class // ug, body, 0)

    def merge(r, c):
        nat[pl.ds(r, cls_len, stride=n_cls), :] = acc[cls_rows(r), :]
        return c

    lax.fori_loop(0, n_cls, merge, 0)
    o_ref[...] = nat[...].astype(BF16)


def _attention(qk, rest, batch, seq):
    assert all(seq % (d * KEYS_BACK) == 0 for _, d in DILATIONS), "every residue class is whole blocks"
    blockspec = lambda off: pl.BlockSpec((None, seq, HEAD_DIM), lambda b, h: (b, 0, off + h))
    return pl.pallas_call(
        functools.partial(_attn_kernel, seq=seq),
        grid=(batch, N_HEADS),
        in_specs=[blockspec(0), blockspec(N_HEADS), blockspec(0)],
        out_specs=blockspec(0),
        out_shape=jax.ShapeDtypeStruct((batch, seq, ATTN_WIDTH), BF16),
        scratch_shapes=[pltpu.VMEM((seq, HEAD_DIM), F32)] * 7,
        compiler_params=_params("parallel", "parallel"),
        name="dilated_attention",
    )(qk, qk, rest)


def _outproj_kernel(h_ref, a_ref, gb_ref, gc_ref, hv_ref, gch_ref, hvh_ref, cw_ref, wo_ref, o_ref,
                    ext, *, tm, seq):
    i = pl.program_id(0)
    first = (i * tm) % seq == 0
    u = gc_ref[...].astype(F32) * hv_ref[...].astype(F32)
    u_halo = gch_ref[...].astype(F32) * hvh_ref[...].astype(F32)
    ext[0:HALO, :] = jnp.where(first, 0.0, u_halo)
    ext[HALO:HALO + tm, :] = u
    w = cw_ref[...]
    conv = w[CONV_K - 1:CONV_K, :] * u
    for back in range(1, CONV_K):
        conv = conv + w[CONV_K - 1 - back:CONV_K - back, :] * ext[pl.ds(HALO - back, tm), :]
    c = (gb_ref[...].astype(F32) * conv).astype(BF16)
    y = jnp.dot(a_ref[...], wo_ref[0:ATTN_WIDTH, :], preferred_element_type=F32)
    y = y + jnp.dot(c, wo_ref[ATTN_WIDTH:, :], preferred_element_type=F32)
    o_ref[...] = h_ref[...] + y


def _outproj(h, attn, proj, conv_w, w_out, seq, tm=512):
    n, d = h.shape
    assert seq % tm == 0 and tm % HALO == 0, "row tiles must not straddle sequences"
    cb = ATTN_WIDTH // CONV_WIDTH
    hb = tm // HALO
    cur = lambda k: pl.BlockSpec((tm, CONV_WIDTH), lambda i: (i, cb + k))
    halo = lambda k: pl.BlockSpec((HALO, CONV_WIDTH), lambda i: (jnp.maximum(i * hb - 1, 0), cb + k))
    return pl.pallas_call(
        functools.partial(_outproj_kernel, tm=tm, seq=seq),
        grid=(n // tm,),
        in_specs=[
            pl.BlockSpec((tm, d), lambda i: (i, 0)),
            pl.BlockSpec((tm, ATTN_WIDTH), lambda i: (i, 0)),
            cur(0), cur(1), cur(2), halo(1), halo(2),
            pl.BlockSpec((CONV_K, CONV_WIDTH), lambda i: (0, 0)),
            pl.BlockSpec((d, d), lambda i: (0, 0)),
        ],
        out_specs=pl.BlockSpec((tm, d), lambda i: (i, 0)),
        out_shape=jax.ShapeDtypeStruct((n, d), F32),
        scratch_shapes=[pltpu.VMEM((HALO + tm, CONV_WIDTH), F32)],
        compiler_params=_params("parallel"),
        name="conv_outproj_residual",
    )(h, attn, proj, proj, proj, proj, proj, conv_w, w_out)


def _ffn_kernel(h_ref, g_ref, wg_ref, wu_ref, wd_ref, o_ref, xn_ref):
    j = pl.program_id(1)

    @pl.when(j == 0)
    def _():
        h = h_ref[...]
        xn_ref[...] = _rms(h, g_ref[...]).astype(BF16)
        o_ref[...] = h

    x = xn_ref[...]
    gate = jnp.dot(x, wg_ref[...], preferred_element_type=F32)
    up = jnp.dot(x, wu_ref[...], preferred_element_type=F32)
    act = (gate * jax.nn.sigmoid(gate) * up).astype(BF16)
    o_ref[...] += jnp.dot(act, wd_ref[...], preferred_element_type=F32)


def _ffn(h, g, wg, wu, wd, tm=1024, tf=512):
    n, d = h.shape
    f = wg.shape[1]
    assert n % tm == 0 and f % tf == 0
    return pl.pallas_call(
        _ffn_kernel,
        grid=(n // tm, f // tf),
        in_specs=[
            pl.BlockSpec((tm, d), lambda i, j: (i, 0)),
            pl.BlockSpec((1, d), lambda i, j: (0, 0)),
            pl.BlockSpec((d, tf), lambda i, j: (0, j)),
            pl.BlockSpec((d, tf), lambda i, j: (0, j)),
            pl.BlockSpec((tf, d), lambda i, j: (j, 0)),
        ],
        out_specs=pl.BlockSpec((tm, d), lambda i, j: (i, 0)),
        out_shape=jax.ShapeDtypeStruct((n, d), F32),
        scratch_shapes=[pltpu.VMEM((tm, d), BF16)],
        compiler_params=_params("parallel", "arbitrary"),
        name="norm_swiglu_residual",
    )(h, g, wg, wu, wd)


ROUTE_E1, ROUTE_E2, ROUTE_R1, ROUTE_R2, ROUTE_G1, ROUTE_G2 = range(6)


def _pool_router_kernel(h_ref, hh_ref, gm_ref, pw_ref, ps_ref, gf_ref, rw_ref,
                        hn_ref, xn_ref, route_ref, cnt_ref, ext, run, *, tm, seq):
    i = pl.program_id(0)

    @pl.when(i == 0)
    def _():
        run[...] = jnp.zeros_like(run)

    pos0 = (i * tm) % seq
    h = h_ref[...]
    gm = gm_ref[...]
    xn = _rms(h, gm)
    ext[0:HALO, :] = jnp.where(pos0 == 0, 0.0, _rms(hh_ref[...], gm))
    ext[HALO:HALO + tm, :] = xn
    t1 = (pos0 + 1 + lax.broadcasted_iota(jnp.int32, (tm, 1), 0)).astype(F32)
    group = h.shape[1] // len(POOL_WINDOWS)
    outs = []
    for gi, w in enumerate(POOL_WINDOWS):
        cols = slice(gi * group, (gi + 1) * group)
        x_g = xn[:, cols]
        assert w & (w - 1) == 0 and w <= HALO
        s = ext[:, cols]
        shift = 1
        while shift < w:
            s = s + pltpu.roll(s, shift, axis=0)
            shift *= 2
        s = s[HALO:, :]
        mixed = s / jnp.minimum(t1, float(w)) - x_g
        outs.append(jnp.dot(mixed.astype(BF16), pw_ref[gi], preferred_element_type=F32))
    hn = h + jnp.concatenate(outs, axis=1) * ps_ref[...]
    hn_ref[...] = hn

    xn2 = _rms(hn, gf_ref[...])
    xn_ref[...] = _pack_bf16_pairs(xn2)
    logits = jnp.dot(xn2, rw_ref[...], preferred_element_type=F32, precision=lax.Precision.HIGHEST)
    lane = lax.broadcasted_iota(jnp.int32, (tm, LANES), 1)
    logits = jnp.where(lane < N_EXPERTS, logits, -jnp.inf)
    m1 = jnp.max(logits, axis=-1, keepdims=True)
    e1 = jnp.min(jnp.where(logits == m1, lane, LANES), axis=-1, keepdims=True)
    rest = jnp.where(lane == e1, -jnp.inf, logits)
    m2 = jnp.max(rest, axis=-1, keepdims=True)
    e2 = jnp.min(jnp.where(rest == m2, lane, LANES), axis=-1, keepdims=True)
    ex = jnp.exp(m2 - m1)
    g1 = 1.0 / (1.0 + ex)
    g2 = ex / (1.0 + ex)

    hit1 = lane == e1
    hit2 = lane == e2
    onehot = jnp.where(hit1 | hit2, 1.0, 0.0)
    tri = (lax.broadcasted_iota(jnp.int32, (tm, tm), 1) < lax.broadcasted_iota(jnp.int32, (tm, tm), 0))
    before = jnp.dot(tri.astype(BF16), onehot.astype(BF16), preferred_element_type=F32) + run[...]
    r1 = jnp.sum(jnp.where(hit1, before, 0.0), axis=-1, keepdims=True)
    r2 = jnp.sum(jnp.where(hit2, before, 0.0), axis=-1, keepdims=True)
    run[...] = run[...] + jnp.sum(onehot, axis=0, keepdims=True)
    cnt_ref[...] = run[...]

    route = jnp.zeros((tm, LANES), F32)
    for slot, val in ((ROUTE_E1, e1.astype(F32)), (ROUTE_E2, e2.astype(F32)), (ROUTE_R1, r1),
                      (ROUTE_R2, r2), (ROUTE_G1, g1), (ROUTE_G2, g2)):
        route = jnp.where(lane == slot, val, route)
    route_ref[...] = route


def _pool_router(h, g_mix, pool_w, pool_scale, g_ffn, router_w, seq, tm=512):
    n, d = h.shape
    assert seq % tm == 0 and tm % HALO == 0, "row tiles must not straddle sequences"
    hb = tm // HALO
    row = pl.BlockSpec((tm, d), lambda i: (i, 0))
    vec = pl.BlockSpec((1, d), lambda i: (0, 0))
    rw = jnp.zeros((d, LANES), F32).at[:, :N_EXPERTS].set(router_w)
    return pl.pallas_call(
        functools.partial(_pool_router_kernel, tm=tm, seq=seq),
        grid=(n // tm,),
        in_specs=[
            row,
            pl.BlockSpec((HALO, d), lambda i: (jnp.maximum(i * hb - 1, 0), 0)),
            vec,
            pl.BlockSpec(pool_w.shape, lambda i: (0, 0, 0)),
            vec, vec,
            pl.BlockSpec((d, LANES), lambda i: (0, 0)),
        ],
        out_specs=[row, pl.BlockSpec((tm, d // 2), lambda i: (i, 0)),
                   pl.BlockSpec((tm, LANES), lambda i: (i, 0)), pl.BlockSpec((1, LANES), lambda i: (0, 0))],
        out_shape=[jax.ShapeDtypeStruct((n, d), F32), jax.ShapeDtypeStruct((n, d // 2), jnp.uint32),
                   jax.ShapeDtypeStruct((n, LANES), F32), jax.ShapeDtypeStruct((1, LANES), F32)],
        scratch_shapes=[pltpu.VMEM((HALO + tm, d), F32), pltpu.VMEM((1, LANES), F32)],
        compiler_params=_params("arbitrary"),
        name="pool_residual_router",
    )(h, h, g_mix, pool_w, pool_scale, g_ffn, rw)


def _row_copy(src_ref, src_row, dst_ref, dst_row, sem):
    return pltpu.make_async_copy(src_ref.at[pl.ds(src_row, 1)], dst_ref.at[pl.ds(dst_row, 1)], sem)


def _dispatch_kernel(d1_ref, d2_ref, x_ref, xs_in_ref, xs_ref, sem, *, tm):
    del xs_in_ref

    def issue(t, c):
        _row_copy(x_ref, t, xs_ref, d1_ref[0, t], sem).start()
        _row_copy(x_ref, t, xs_ref, d2_ref[0, t], sem).start()
        return c

    lax.fori_loop(0, tm, issue, 0, unroll=ROW_DMA_UNROLL)

    def drain(t, c):
        _row_copy(x_ref, 0, xs_ref, 0, sem).wait()
        _row_copy(x_ref, 0, xs_ref, 0, sem).wait()
        return c

    lax.fori_loop(0, tm, drain, 0, unroll=ROW_DMA_UNROLL)


def _dispatch(xn, dest1, dest2, n_rows, tm=512):
    n, w = xn.shape
    idx = pl.BlockSpec((None, 1, tm), lambda i: (i, 0, 0), memory_space=pltpu.SMEM)
    return pl.pallas_call(
        functools.partial(_dispatch_kernel, tm=tm),
        grid=(n // tm,),
        in_specs=[idx, idx, pl.BlockSpec((tm, w), lambda i: (i, 0)), pl.BlockSpec(memory_space=pl.ANY)],
        out_specs=pl.BlockSpec(memory_space=pl.ANY),
        out_shape=jax.ShapeDtypeStruct((n_rows, w), xn.dtype),
        scratch_shapes=[pltpu.SemaphoreType.DMA(())],
        input_output_aliases={3: 0},
        compiler_params=_params("arbitrary"),
        name="expert_dispatch",
    )(dest1.reshape(n // tm, 1, tm), dest2.reshape(n // tm, 1, tm), xn, jnp.zeros((n_rows, w), xn.dtype))


def _moe_kernel(te_ref, tr_ref, x_ref, wg_ref, wu_ref, wd_ref, o_ref, xb_ref):
    del te_ref
    i = pl.program_id(0)
    j = pl.program_id(1)
    rows = tr_ref[i]
    half = x_ref.shape[1]

    @pl.when(j == 0)
    def _():
        o_ref[...] = jnp.zeros_like(o_ref)

    @pl.when((rows > 0) & (j == 0))
    def _():
        lo, hi = _unpack_bf16_pairs(x_ref[...])
        xb_ref[:, :half] = lo
        xb_ref[:, half:] = hi

    def swiglu_top_rows(n_rows):
        r = pl.ds(0, n_rows)
        x = xb_ref[r, :]
        gate = jnp.dot(x, wg_ref[...].astype(BF16), preferred_element_type=F32)
        up = jnp.dot(x, wu_ref[...].astype(BF16), preferred_element_type=F32)
        act = (gate * jax.nn.sigmoid(gate) * up).astype(BF16)
        o_ref[r, :] += jnp.dot(act, wd_ref[...].astype(BF16), preferred_element_type=F32)

    tm = o_ref.shape[0]
    pl.when(rows > tm // 2)(lambda: swiglu_top_rows(tm))
    pl.when((rows > tm // 4) & (rows <= tm // 2))(lambda: swiglu_top_rows(tm // 2))
    pl.when((rows > 0) & (rows <= tm // 4))(lambda: swiglu_top_rows(tm // 4))


def _moe(xs, tile_expert, tile_rows, layer, wg, wu, wd, tm, tf=256):
    p, half = xs.shape
    d = 2 * half
    f = wg.shape[3]
    nj = f // tf
    assert p % tm == 0 and f % tf == 0

    def col(i, j, tr):
        return jnp.where(tr[i] > 0, j, nj - 1)

    grid_spec = pltpu.PrefetchScalarGridSpec(
        num_scalar_prefetch=2,
        grid=(p // tm, nj),
        in_specs=[
            pl.BlockSpec((tm, half), lambda i, j, te, tr: (jnp.where(tr[i] > 0, i, 0), 0)),
            pl.BlockSpec((None, None, d, tf), lambda i, j, te, tr: (layer, te[i], 0, col(i, j, tr))),
            pl.BlockSpec((None, None, d, tf), lambda i, j, te, tr: (layer, te[i], 0, col(i, j, tr))),
            pl.BlockSpec((None, None, tf, d), lambda i, j, te, tr: (layer, te[i], col(i, j, tr), 0)),
        ],
        out_specs=pl.BlockSpec((tm, d), lambda i, j, te, tr: (i, 0)),
        scratch_shapes=[pltpu.VMEM((tm, d), BF16)],
    )
    return pl.pallas_call(
        _moe_kernel,
        grid_spec=grid_spec,
        out_shape=jax.ShapeDtypeStruct((p, d), F32),
        compiler_params=_params("arbitrary", "arbitrary"),
        name="grouped_expert_swiglu",
    )(tile_expert, tile_rows, xs, wg, wu, wd)


def _combine_kernel(d1_ref, d2_ref, h_ref, route_ref, gn_ref, ys_ref, o_ref, b1, b2, sem, *, tm, final_norm):
    def issue(t, c):
        _row_copy(ys_ref, d1_ref[0, t], b1, t, sem).start()
        _row_copy(ys_ref, d2_ref[0, t], b2, t, sem).start()
        return c

    lax.fori_loop(0, tm, issue, 0, unroll=ROW_DMA_UNROLL)

    def drain(t, c):
        _row_copy(ys_ref, 0, b1, 0, sem).wait()
        _row_copy(ys_ref, 0, b2, 0, sem).wait()
        return c

    lax.fori_loop(0, tm, drain, 0, unroll=ROW_DMA_UNROLL)

    route = route_ref[...]
    g1 = route[:, ROUTE_G1:ROUTE_G1 + 1]
    g2 = route[:, ROUTE_G2:ROUTE_G2 + 1]
    out = h_ref[...] + (g1 * b1[...] + g2 * b2[...])
    if final_norm:
        out = _rms(out, gn_ref[...])
    o_ref[...] = out


def _combine(h, route, dest1, dest2, ys, g_final, final_norm, tm=512):
    n, d = h.shape
    idx = pl.BlockSpec((None, 1, tm), lambda i: (i, 0, 0), memory_space=pltpu.SMEM)
    row = pl.BlockSpec((tm, d), lambda i: (i, 0))
    return pl.pallas_call(
        functools.partial(_combine_kernel, tm=tm, final_norm=final_norm),
        grid=(n // tm,),
        in_specs=[idx, idx, row, pl.BlockSpec((tm, LANES), lambda i: (i, 0)),
                  pl.BlockSpec((1, d), lambda i: (0, 0)), pl.BlockSpec(memory_space=pl.ANY)],
        out_specs=row,
        out_shape=jax.ShapeDtypeStruct((n, d), F32),
        scratch_shapes=[pltpu.VMEM((tm, d), F32), pltpu.VMEM((tm, d), F32), pltpu.SemaphoreType.DMA(())],
        compiler_params=_params("arbitrary"),
        name="expert_combine_residual",
    )(dest1.reshape(n // tm, 1, tm), dest2.reshape(n // tm, 1, tm), h, route, g_final, ys)


def _moe_layer(h, route, counts, xn, layer, wg, wu, wd, g_final, final_norm, tm=1024):
    n, d = h.shape
    n_tiles_max = 2 * n // tm + N_EXPERTS
    counts = counts[0, :N_EXPERTS].astype(jnp.int32)
    padded = (counts + tm - 1) // tm * tm
    ends = jnp.cumsum(padded)
    offs = ends - padded
    col = lambda k: route[:, k].astype(jnp.int32)
    dest1 = offs[col(ROUTE_E1)] + col(ROUTE_R1)
    dest2 = offs[col(ROUTE_E2)] + col(ROUTE_R2)
    tile_start = jnp.arange(n_tiles_max, dtype=jnp.int32) * tm
    last_start = jnp.minimum(tile_start, ends[-1] - tm)
    tile_expert = jnp.sum((last_start[:, None] >= ends[None, :]).astype(jnp.int32), axis=1)
    tile_rows = jnp.clip((offs + counts)[tile_expert] - tile_start, 0, tm).astype(jnp.int32)
    xs = _dispatch(xn, dest1, dest2, n_tiles_max * tm)
    ys = _moe(xs, tile_expert, tile_rows, layer, wg, wu, wd, tm)
    return _combine(h, route, dest1, dest2, ys, g_final, final_norm)


def kernel(x, positions, norm_mix_even, w_in, conv_w, w_out, norm_ffn_even, w_gate, w_up, w_down,
           norm_mix_odd, pool_w, pool_scale, norm_ffn_odd, router_w, exp_w_gate, exp_w_up, exp_w_down,
           final_norm):
    batch, seq, d = x.shape
    n = batch * seq
    depth = norm_mix_even.shape[0] + norm_mix_odd.shape[0]
    assert depth % 2 == 0, "the final norm is fused into the last (odd) layer's combine"
    h = x.reshape(n, d)
    cos, sin = _rope_tables(positions)
    g_final = final_norm[None, :]
    for layer in range(depth):
        i = layer // 2
        if layer % 2 == 0:
            g_mix, w_in_b = norm_mix_even[i][None, :], w_in[i].astype(BF16)
            n_rot = 2 * ATTN_WIDTH
            qk = _inproj(h, g_mix, w_in_b, cos, sin, 0, n_rot, True)
            rest = _inproj(h, g_mix, w_in_b, cos, sin, n_rot, w_in_b.shape[1] - n_rot, False)
            attn = _attention(qk.reshape(batch, seq, -1), rest.reshape(batch, seq, -1), batch, seq)
            h = _outproj(h, attn.reshape(n, ATTN_WIDTH), rest, conv_w[i], w_out[i].astype(BF16), seq)
            h = _ffn(h, norm_ffn_even[i][None, :], w_gate[i].astype(BF16), w_up[i].astype(BF16),
                     w_down[i].astype(BF16))
        else:
            h, xn, route, counts = _pool_router(h, norm_mix_odd[i][None, :], pool_w[i].astype(BF16),
                                                pool_scale[i][None, :], norm_ffn_odd[i][None, :],
                                                router_w[i], seq)
            h = _moe_layer(h, route, counts, xn, i, exp_w_gate, exp_w_up, exp_w_down, g_final,
                           layer == depth - 1)
    return h.reshape(batch, seq, d)
```

```python
import functools
import math

import jax
import jax.numpy as jnp
from jax import lax
from jax.experimental import pallas as pl
from jax.experimental.pallas import tpu as pltpu

F32 = jnp.float32
BF16 = jnp.bfloat16

HEAD_DIM = 128
N_HEADS = 12
ATTN_WIDTH = N_HEADS * HEAD_DIM
CONV_WIDTH = 512
CONV_K = 3
DILATIONS = ((128, 1), (512, 4), (2048, 16))
KEYS_BACK = 128
ROPE_THETA = 10000.0
POOL_WINDOWS = (2, 4, 8, 16)
N_EXPERTS = 8
RMS_EPS = 1e-5
NEG_INF = -1e30
LANES = 128
HALO = 16
VMEM_LIMIT = 56 * 1024 * 1024
ATTN_TILE_BLOCKS = 4
ATTN_BASE_DILATION = 4
ATTN_INTERLEAVE = 16
ROW_DMA_UNROLL = 8

def _params(*sem):
    return pltpu.CompilerParams(dimension_semantics=sem, vmem_limit_bytes=VMEM_LIMIT)


def _rms(x, g):
    return x * lax.rsqrt(jnp.mean(x * x, axis=-1, keepdims=True) + RMS_EPS) * g


def _pack_bf16_pairs(x):
    half = x.shape[1] // 2
    bits = lax.bitcast_convert_type(x.astype(BF16).astype(F32), jnp.uint32)
    return (bits[:, :half] >> 16) | (bits[:, half:] & jnp.uint32(0xFFFF0000))


def _unpack_bf16_pairs(words):
    lo = lax.bitcast_convert_type(words << 16, F32).astype(BF16)
    hi = lax.bitcast_convert_type(words & jnp.uint32(0xFFFF0000), F32).astype(BF16)
    return lo, hi


def _rope_kernel(pos_ref, freq_ref, sign_ref, cos_ref, sin_ref):
    ang = pos_ref[...].astype(F32) * freq_ref[...]
    cos_ref[...] = jnp.cos(ang)
    sin_ref[...] = jnp.sin(ang) * sign_ref[...]


def _rope_tables(positions, tm=1024):
    n = positions.size
    half = HEAD_DIM // 2
    inv_freq = jnp.power(ROPE_THETA, -jnp.arange(half, dtype=F32) / half)
    freq = jnp.concatenate([inv_freq, inv_freq])[None, :]
    sign = jnp.concatenate([-jnp.ones((half,), F32), jnp.ones((half,), F32)])[None, :]
    row = pl.BlockSpec((tm, HEAD_DIM), lambda i: (i, 0))
    const = pl.BlockSpec((1, HEAD_DIM), lambda i: (0, 0))
    return pl.pallas_call(
        _rope_kernel,
        grid=(n // tm,),
        in_specs=[pl.BlockSpec((tm, 1), lambda i: (i, 0)), const, const],
        out_specs=[row, row],
        out_shape=[jax.ShapeDtypeStruct((n, HEAD_DIM), F32)] * 2,
        compiler_params=_params("parallel"),
        name="rope_tables",
    )(positions.reshape(n, 1), freq, sign)


def _inproj_kernel(h_ref, g_ref, w_ref, cos_ref, sin_ref, o_ref, xn_ref, *, rotary, tn):
    j = pl.program_id(1)

    @pl.when(j == 0)
    def _():
        xn_ref[...] = _rms(h_ref[...], g_ref[...]).astype(BF16)

    y = jnp.dot(xn_ref[...], w_ref[...], preferred_element_type=F32)
    if rotary:
        cos = cos_ref[...]
        sin = sin_ref[...]
        for c in range(tn // HEAD_DIM):
            t = y[:, c * HEAD_DIM:(c + 1) * HEAD_DIM]
            rot = t * cos + pltpu.roll(t, HEAD_DIM // 2, axis=1) * sin
            o_ref[:, c * HEAD_DIM:(c + 1) * HEAD_DIM] = rot.astype(BF16)
    else:
        o_ref[...] = y.astype(BF16)


def _inproj(h, g, w, cos, sin, col0, width, rotary, tm=1024, tn=1536):
    n, d = h.shape
    assert width % tn == 0 and col0 % tn == 0 and n % tm == 0
    jb = col0 // tn
    return pl.pallas_call(
        functools.partial(_inproj_kernel, rotary=rotary, tn=tn),
        grid=(n // tm, width // tn),
        in_specs=[
            pl.BlockSpec((tm, d), lambda i, j: (i, 0)),
            pl.BlockSpec((1, d), lambda i, j: (0, 0)),
            pl.BlockSpec((d, tn), lambda i, j: (0, jb + j)),
            pl.BlockSpec((tm, HEAD_DIM), lambda i, j: (i, 0)),
            pl.BlockSpec((tm, HEAD_DIM), lambda i, j: (i, 0)),
        ],
        out_specs=pl.BlockSpec((tm, tn), lambda i, j: (i, j)),
        out_shape=jax.ShapeDtypeStruct((n, width), BF16),
        scratch_shapes=[pltpu.VMEM((tm, d), BF16)],
        compiler_params=_params("parallel", "arbitrary"),
        name="norm_inproj_rotary" if rotary else "norm_inproj_plain",
    )(h, g, w, cos, sin)


def _attn_kernel(q_ref, k_ref, v_ref, o_ref, nat, q16, k16, v16, acc, m_s, l_s, *, seq):
    blk = KEYS_BACK
    n_cls = ATTN_BASE_DILATION
    cls_len = seq // n_cls
    scale = HEAD_DIM ** -0.5 * math.log2(math.e)
    last = len(DILATIONS) - 1
    nt = (((1,), (1,)), ((), ()))
    u = ATTN_INTERLEAVE

    def cls_rows(r):
        return pl.ds(pl.multiple_of(r * cls_len, cls_len), cls_len)

    for src_ref, dst in ((q_ref, q16), (k_ref, k16), (v_ref, v16)):
        nat[...] = src_ref[...].astype(F32)

        def split(r, c, dst=dst):
            dst[cls_rows(r), :] = nat[pl.ds(r, cls_len, stride=n_cls), :]
            return c

        lax.fori_loop(0, n_cls, split, 0)

    for gi, (window, d) in enumerate(DILATIONS):
        assert window // d == blk and (n_cls % d == 0 or d % n_cls == 0)
        n_blk = seq // d // blk
        nb = min(ATTN_TILE_BLOCKS, n_blk)
        tiles_per_class = n_blk // nb
        pieces = max(n_cls // d, 1)
        stride = max(d // n_cls, 1)
        plen = blk // pieces
        idx0 = lax.broadcasted_iota(jnp.int32, (blk, blk), 0)
        idx1 = lax.broadcasted_iota(jnp.int32, (blk, blk), 1)
        pos0 = pieces * (idx0 % plen) + idx0 // plen
        pos1 = pieces * (idx1 % plen) + idx1 // plen
        mask_cur = pos1 <= pos0
        mask_prev = pos1 >= pos0
        mask_both = jnp.concatenate([mask_prev, mask_cur], axis=1)

        def piece_rows(rc, b, c, d=d, plen=plen, stride=stride):
            if stride > 1:
                return pl.ds((rc % n_cls) * cls_len + rc // n_cls + stride * blk * b, blk, stride=stride)
            return pl.ds(pl.multiple_of((d * c + rc) * cls_len + plen * b, 8), plen)

        def load_block(ref, rc, b, pieces=pieces, piece_rows=piece_rows):
            return jnp.concatenate([ref[piece_rows(rc, b, c), :] for c in range(pieces)], axis=0)

        def store_block(ref, rc, b, val, pieces=pieces, plen=plen, piece_rows=piece_rows):
            for c in range(pieces):
                ref[piece_rows(rc, b, c), :] = val[c * plen:(c + 1) * plen, :]

        def load_tile(rc, b0, first, gi=gi, nb=nb, load_block=load_block):
            ko = 0 if first else 1
            t = dict(rc=rc, b0=b0, ko=ko,
                     q=[load_block(q16, rc, b0 + b).astype(BF16) for b in range(nb)],
                     k=[load_block(k16, rc, b0 + b - ko).astype(BF16) for b in range(nb + ko)],
                     v=[load_block(v16, rc, b0 + b - ko).astype(BF16) for b in range(nb + ko)])
            if gi > 0:
                t.update(m=[load_block(m_s, rc, b0 + b) for b in range(nb)],
                         l=[load_block(l_s, rc, b0 + b) for b in range(nb)],
                         a=[load_block(acc, rc, b0 + b) for b in range(nb)])
            return t

        def compute_tile(t, gi=gi, nb=nb, mask_cur=mask_cur, mask_both=mask_both):
            out = []
            for b in range(nb):
                own = b + t["ko"]
                if own > 0:
                    k = jnp.concatenate([t["k"][own - 1], t["k"][own]], axis=0)
                    v = jnp.concatenate([t["v"][own - 1], t["v"][own]], axis=0)
                    mask = mask_both
                else:
                    k, v, mask = t["k"][own], t["v"][own], mask_cur
                s = lax.dot_general(t["q"][b], k, nt, preferred_element_type=F32)
                s = jnp.where(mask, s * scale, NEG_INF)
                m_blk = jnp.max(s, axis=-1, keepdims=True)
                if gi == 0:
                    m_new = jnp.broadcast_to(m_blk, (blk, blk))
                else:
                    m_old = t["m"][b]
                    m_new = jnp.maximum(m_old, m_blk)
                p_parts = [jnp.exp2(s[:, c * blk:(c + 1) * blk] - m_new) for c in range(s.shape[1] // blk)]
                p_sum = p_parts[0] if len(p_parts) == 1 else p_parts[0] + p_parts[1]
                l_new = jnp.broadcast_to(jnp.sum(p_sum, axis=-1, keepdims=True), (blk, blk))
                p = jnp.concatenate([x.astype(BF16) for x in p_parts], axis=1)
                pv = jnp.dot(p, v, preferred_element_type=F32)
                if gi > 0:
                    alpha = jnp.exp2(m_old - m_new)
                    l_new = alpha * t["l"][b] + l_new
                    pv = alpha * t["a"][b] + pv
                if gi == last:
                    pv = pv / l_new
                out.append((pv, m_new, l_new))
            return out

        def run_tiles(tiles, gi=gi, nb=nb, load_tile=load_tile, compute_tile=compute_tile,
                      store_block=store_block):
            loaded = [load_tile(rc, b0, first) for rc, b0, first in tiles]
            results = [compute_tile(t) for t in loaded]
            for t, res in zip(loaded, results):
                for b, (pv, m_new, l_new) in enumerate(res):
                    store_block(acc, t["rc"], t["b0"] + b, pv)
                    if gi < last:
                        store_block(m_s, t["rc"], t["b0"] + b, m_new)
                        store_block(l_s, t["rc"], t["b0"] + b, l_new)

        ug = min(u, d * tiles_per_class)
        if tiles_per_class <= ug:
            classes = ug // tiles_per_class
            assert ug % tiles_per_class == 0 and d % classes == 0

            def body(it, c, run_tiles=run_tiles, nb=nb, classes=classes, tpc=tiles_per_class):
                run_tiles([(it * classes + cl, k * nb, k == 0) for cl in range(classes) for k in range(tpc)])
                return c

            if d == classes:
                body(0, 0)
            else:
                lax.fori_loop(0, d // classes, body, 0)
        else:
            assert d == 1 and tiles_per_class % ug == 0
            run_tiles([(0, k * nb, k == 0) for k in range(ug)])

            def body(it, c, run_tiles=run_tiles, nb=nb, ug=ug):
                run_tiles([(0, (it * ug + k) * nb, False) for k in range(ug)])
                return c

            lax.fori_loop(1, tiles_per_class // ug, body, 0)

    def merge(r, c):
        nat[pl.ds(r, cls_len, stride=n_cls), :] = acc[cls_rows(r), :]
        return c

    lax.fori_loop(0, n_cls, merge, 0)
    o_ref[...] = nat[...].astype(BF16)


def _attention(qk, rest, batch, seq):
    assert all(seq % (d * KEYS_BACK) == 0 for _, d in DILATIONS), "every residue class is whole blocks"
    blockspec = lambda off: pl.BlockSpec((None, seq, HEAD_DIM), lambda b, h: (b, 0, off + h))
    return pl.pallas_call(
        functools.partial(_attn_kernel, seq=seq),
        grid=(batch, N_HEADS),
        in_specs=[blockspec(0), blockspec(N_HEADS), blockspec(0)],
        out_specs=blockspec(0),
        out_shape=jax.ShapeDtypeStruct((batch, seq, ATTN_WIDTH), BF16),
        scratch_shapes=[pltpu.VMEM((seq, HEAD_DIM), F32)] * 7,
        compiler_params=_params("parallel", "parallel"),
        name="dilated_attention",
    )(qk, qk, rest)


def _outproj_kernel(h_ref, a_ref, gb_ref, gc_ref, hv_ref, gch_ref, hvh_ref, cw_ref, wo_ref, o_ref,
                    ext, *, tm, seq):
    i = pl.program_id(0)
    first = (i * tm) % seq == 0
    u = gc_ref[...].astype(F32) * hv_ref[...].astype(F32)
    u_halo = gch_ref[...].astype(F32) * hvh_ref[...].astype(F32)
    ext[0:HALO, :] = jnp.where(first, 0.0, u_halo)
    ext[HALO:HALO + tm, :] = u
    w = cw_ref[...]
    conv = w[CONV_K - 1:CONV_K, :] * u
    for back in range(1, CONV_K):
        conv = conv + w[CONV_K - 1 - back:CONV_K - back, :] * ext[pl.ds(HALO - back, tm), :]
    c = (gb_ref[...].astype(F32) * conv).astype(BF16)
    y = jnp.dot(a_ref[...], wo_ref[0:ATTN_WIDTH, :], preferred_element_type=F32)
    y = y + jnp.dot(c, wo_ref[ATTN_WIDTH:, :], preferred_element_type=F32)
    o_ref[...] = h_ref[...] + y


def _outproj(h, attn, proj, conv_w, w_out, seq, tm=512):
    n, d = h.shape
    assert seq % tm == 0 and tm % HALO == 0, "row tiles must not straddle sequences"
    cb = ATTN_WIDTH // CONV_WIDTH
    hb = tm // HALO
    cur = lambda k: pl.BlockSpec((tm, CONV_WIDTH), lambda i: (i, cb + k))
    halo = lambda k: pl.BlockSpec((HALO, CONV_WIDTH), lambda i: (jnp.maximum(i * hb - 1, 0), cb + k))
    return pl.pallas_call(
        functools.partial(_outproj_kernel, tm=tm, seq=seq),
        grid=(n // tm,),
        in_specs=[
            pl.BlockSpec((tm, d), lambda i: (i, 0)),
            pl.BlockSpec((tm, ATTN_WIDTH), lambda i: (i, 0)),
            cur(0), cur(1), cur(2), halo(1), halo(2),
            pl.BlockSpec((CONV_K, CONV_WIDTH), lambda i: (0, 0)),
            pl.BlockSpec((d, d), lambda i: (0, 0)),
        ],
        out_specs=pl.BlockSpec((tm, d), lambda i: (i, 0)),
        out_shape=jax.ShapeDtypeStruct((n, d), F32),
        scratch_shapes=[pltpu.VMEM((HALO + tm, CONV_WIDTH), F32)],
        compiler_params=_params("parallel"),
        name="conv_outproj_residual",
    )(h, attn, proj, proj, proj, proj, proj, conv_w, w_out)


def _ffn_kernel(h_ref, g_ref, wg_ref, wu_ref, wd_ref, o_ref, xn_ref):
    j = pl.program_id(1)

    @pl.when(j == 0)
    def _():
        h = h_ref[...]
        xn_ref[...] = _rms(h, g_ref[...]).astype(BF16)
        o_ref[...] = h

    x = xn_ref[...]
    gate = jnp.dot(x, wg_ref[...], preferred_element_type=F32)
    up = jnp.dot(x, wu_ref[...], preferred_element_type=F32)
    act = (gate * jax.nn.sigmoid(gate) * up).astype(BF16)
    o_ref[...] += jnp.dot(act, wd_ref[...], preferred_element_type=F32)


def _ffn(h, g, wg, wu, wd, tm=1024, tf=512):
    n, d = h.shape
    f = wg.shape[1]
    assert n % tm == 0 and f % tf == 0
    return pl.pallas_call(
        _ffn_kernel,
        grid=(n // tm, f // tf),
        in_specs=[
            pl.BlockSpec((tm, d), lambda i, j: (i, 0)),
            pl.BlockSpec((1, d), lambda i, j: (0, 0)),
            pl.BlockSpec((d, tf), lambda i, j: (0, j)),
            pl.BlockSpec((d, tf), lambda i, j: (0, j)),
            pl.BlockSpec((tf, d), lambda i, j: (j, 0)),
        ],
        out_specs=pl.BlockSpec((tm, d), lambda i, j: (i, 0)),
        out_shape=jax.ShapeDtypeStruct((n, d), F32),
        scratch_shapes=[pltpu.VMEM((tm, d), BF16)],
        compiler_params=_params("parallel", "arbitrary"),
        name="norm_swiglu_residual",
    )(h, g, wg, wu, wd)


ROUTE_E1, ROUTE_E2, ROUTE_R1, ROUTE_R2, ROUTE_G1, ROUTE_G2 = range(6)


def _pool_router_kernel(h_ref, hh_ref, gm_ref, pw_ref, ps_ref, gf_ref, rw_ref,
                        hn_ref, xn_ref, route_ref, cnt_ref, ext, run, *, tm, seq):
    i = pl.program_id(0)

    @pl.when(i == 0)
    def _():
        run[...] = jnp.zeros_like(run)

    pos0 = (i * tm) % seq
    h = h_ref[...]
    gm = gm_ref[...]
    xn = _rms(h, gm)
    ext[0:HALO, :] = jnp.where(pos0 == 0, 0.0, _rms(hh_ref[...], gm))
    ext[HALO:HALO + tm, :] = xn
    t1 = (pos0 + 1 + lax.broadcasted_iota(jnp.int32, (tm, 1), 0)).astype(F32)
    group = h.shape[1] // len(POOL_WINDOWS)
    outs = []
    for gi, w in enumerate(POOL_WINDOWS):
        cols = slice(gi * group, (gi + 1) * group)
        x_g = xn[:, cols]
        assert w & (w - 1) == 0 and w <= HALO
        s = ext[:, cols]
        shift = 1
        while shift < w:
            s = s + pltpu.roll(s, shift, axis=0)
            shift *= 2
        s = s[HALO:, :]
        mixed = s / jnp.minimum(t1, float(w)) - x_g
        outs.append(jnp.dot(mixed.astype(BF16), pw_ref[gi], preferred_element_type=F32))
    hn = h + jnp.concatenate(outs, axis=1) * ps_ref[...]
    hn_ref[...] = hn

    xn2 = _rms(hn, gf_ref[...])
    xn_ref[...] = _pack_bf16_pairs(xn2)
    logits = jnp.dot(xn2, rw_ref[...], preferred_element_type=F32, precision=lax.Precision.HIGHEST)
    lane = lax.broadcasted_iota(jnp.int32, (tm, LANES), 1)
    logits = jnp.where(lane < N_EXPERTS, logits, -jnp.inf)
    m1 = jnp.max(logits, axis=-1, keepdims=True)
    e1 = jnp.min(jnp.where(logits == m1, lane, LANES), axis=-1, keepdims=True)
    rest = jnp.where(lane == e1, -jnp.inf, logits)
    m2 = jnp.max(rest, axis=-1, keepdims=True)
    e2 = jnp.min(jnp.where(rest == m2, lane, LANES), axis=-1, keepdims=True)
    ex = jnp.exp(m2 - m1)
    g1 = 1.0 / (1.0 + ex)
    g2 = ex / (1.0 + ex)

    hit1 = lane == e1
    hit2 = lane == e2
    onehot = jnp.where(hit1 | hit2, 1.0, 0.0)
    tri = (lax.broadcasted_iota(jnp.int32, (tm, tm), 1) < lax.broadcasted_iota(jnp.int32, (tm, tm), 0))
    before = jnp.dot(tri.astype(BF16), onehot.astype(BF16), preferred_element_type=F32) + run[...]
    r1 = jnp.sum(jnp.where(hit1, before, 0.0), axis=-1, keepdims=True)
    r2 = jnp.sum(jnp.where(hit2, before, 0.0), axis=-1, keepdims=True)
    run[...] = run[...] + jnp.sum(onehot, axis=0, keepdims=True)
    cnt_ref[...] = run[...]

    route = jnp.zeros((tm, LANES), F32)
    for slot, val in ((ROUTE_E1, e1.astype(F32)), (ROUTE_E2, e2.astype(F32)), (ROUTE_R1, r1),
                      (ROUTE_R2, r2), (ROUTE_G1, g1), (ROUTE_G2, g2)):
        route = jnp.where(lane == slot, val, route)
    route_ref[...] = route


def _pool_router(h, g_mix, pool_w, pool_scale, g_ffn, router_w, seq, tm=512):
    n, d = h.shape
    assert seq % tm == 0 and tm % HALO == 0, "row tiles must not straddle sequences"
    hb = tm // HALO
    row = pl.BlockSpec((tm, d), lambda i: (i, 0))
    vec = pl.BlockSpec((1, d), lambda i: (0, 0))
    rw = jnp.zeros((d, LANES), F32).at[:, :N_EXPERTS].set(router_w)
    return pl.pallas_call(
        functools.partial(_pool_router_kernel, tm=tm, seq=seq),
        grid=(n // tm,),
        in_specs=[
            row,
            pl.BlockSpec((HALO, d), lambda i: (jnp.maximum(i * hb - 1, 0), 0)),
            vec,
            pl.BlockSpec(pool_w.shape, lambda i: (0, 0, 0)),
            vec, vec,
            pl.BlockSpec((d, LANES), lambda i: (0, 0)),
        ],
        out_specs=[row, pl.BlockSpec((tm, d // 2), lambda i: (i, 0)),
                   pl.BlockSpec((tm, LANES), lambda i: (i, 0)), pl.BlockSpec((1, LANES), lambda i: (0, 0))],
        out_shape=[jax.ShapeDtypeStruct((n, d), F32), jax.ShapeDtypeStruct((n, d // 2), jnp.uint32),
                   jax.ShapeDtypeStruct((n, LANES), F32), jax.ShapeDtypeStruct((1, LANES), F32)],
        scratch_shapes=[pltpu.VMEM((HALO + tm, d), F32), pltpu.VMEM((1, LANES), F32)],
        compiler_params=_params("arbitrary"),
        name="pool_residual_router",
    )(h, h, g_mix, pool_w, pool_scale, g_ffn, rw)


def _row_copy(src_ref, src_row, dst_ref, dst_row, sem):
    return pltpu.make_async_copy(src_ref.at[pl.ds(src_row, 1)], dst_ref.at[pl.ds(dst_row, 1)], sem)


def _dispatch_kernel(d1_ref, d2_ref, x_ref, xs_in_ref, xs_ref, sem, *, tm):
    del xs_in_ref

    def issue(t, c):
        _row_copy(x_ref, t, xs_ref, d1_ref[0, t], sem).start()
        _row_copy(x_ref, t, xs_ref, d2_ref[0, t], sem).start()
        return c

    lax.fori_loop(0, tm, issue, 0, unroll=ROW_DMA_UNROLL)

    def drain(t, c):
        _row_copy(x_ref, 0, xs_ref, 0, sem).wait()
        _row_copy(x_ref, 0, xs_ref, 0, sem).wait()
        return c

    lax.fori_loop(0, tm, drain, 0, unroll=ROW_DMA_UNROLL)


def _dispatch(xn, dest1, dest2, n_rows, tm=512):
    n, w = xn.shape
    idx = pl.BlockSpec((None, 1, tm), lambda i: (i, 0, 0), memory_space=pltpu.SMEM)
    return pl.pallas_call(
        functools.partial(_dispatch_kernel, tm=tm),
        grid=(n // tm,),
        in_specs=[idx, idx, pl.BlockSpec((tm, w), lambda i: (i, 0)), pl.BlockSpec(memory_space=pl.ANY)],
        out_specs=pl.BlockSpec(memory_space=pl.ANY),
        out_shape=jax.ShapeDtypeStruct((n_rows, w), xn.dtype),
        scratch_shapes=[pltpu.SemaphoreType.DMA(())],
        input_output_aliases={3: 0},
        compiler_params=_params("arbitrary"),
        name="expert_dispatch",
    )(dest1.reshape(n // tm, 1, tm), dest2.reshape(n // tm, 1, tm), xn, jnp.zeros((n_rows, w), xn.dtype))


def _moe_kernel(te_ref, tr_ref, x_ref, wg_ref, wu_ref, wd_ref, o_ref, xb_ref):
    del te_ref
    i = pl.program_id(0)
    j = pl.program_id(1)
    rows = tr_ref[i]
    half = x_ref.shape[1]

    @pl.when(j == 0)
    def _():
        o_ref[...] = jnp.zeros_like(o_ref)

    @pl.when((rows > 0) & (j == 0))
    def _():
        lo, hi = _unpack_bf16_pairs(x_ref[...])
        xb_ref[:, :half] = lo
        xb_ref[:, half:] = hi

    def swiglu_top_rows(n_rows):
        r = pl.ds(0, n_rows)
        x = xb_ref[r, :]
        gate = jnp.dot(x, wg_ref[...].astype(BF16), preferred_element_type=F32)
        up = jnp.dot(x, wu_ref[...].astype(BF16), preferred_element_type=F32)
        act = (gate * jax.nn.sigmoid(gate) * up).astype(BF16)
        o_ref[r, :] += jnp.dot(act, wd_ref[...].astype(BF16), preferred_element_type=F32)

    tm = o_ref.shape[0]
    pl.when(rows > tm // 2)(lambda: swiglu_top_rows(tm))
    pl.when((rows > tm // 4) & (rows <= tm // 2))(lambda: swiglu_top_rows(tm // 2))
    pl.when((rows > 0) & (rows <= tm // 4))(lambda: swiglu_top_rows(tm // 4))


def _moe(xs, tile_expert, tile_rows, layer, wg, wu, wd, tm, tf=512):
    p, half = xs.shape
    d = 2 * half
    f = wg.shape[3]
    nj = f // tf
    assert p % tm == 0 and f % tf == 0

    def col(i, j, tr):
        return jnp.where(tr[i] > 0, j, nj - 1)

    grid_spec = pltpu.PrefetchScalarGridSpec(
        num_scalar_prefetch=2,
        grid=(p // tm, nj),
        in_specs=[
            pl.BlockSpec((tm, half), lambda i, j, te, tr: (jnp.where(tr[i] > 0, i, 0), 0),
                         pipeline_mode=pl.Buffered(1)),
            pl.BlockSpec((None, None, d, tf), lambda i, j, te, tr: (layer, te[i], 0, col(i, j, tr))),
            pl.BlockSpec((None, None, d, tf), lambda i, j, te, tr: (layer, te[i], 0, col(i, j, tr))),
            pl.BlockSpec((None, None, tf, d), lambda i, j, te, tr: (layer, te[i], col(i, j, tr), 0)),
        ],
        out_specs=pl.BlockSpec((tm, d), lambda i, j, te, tr: (i, 0)),
        scratch_shapes=[pltpu.VMEM((tm, d), BF16)],
    )
    return pl.pallas_call(
        _moe_kernel,
        grid_spec=grid_spec,
        out_shape=jax.ShapeDtypeStruct((p, d), F32),
        compiler_params=_params("arbitrary", "arbitrary"),
        name="grouped_expert_swiglu",
    )(tile_expert, tile_rows, xs, wg, wu, wd)


def _combine_kernel(d1_ref, d2_ref, h_ref, route_ref, gn_ref, ys_ref, o_ref, b1, b2, sem, *, tm, final_norm):
    def issue(t, c):
        _row_copy(ys_ref, d1_ref[0, t], b1, t, sem).start()
        _row_copy(ys_ref, d2_ref[0, t], b2, t, sem).start()
        return c

    lax.fori_loop(0, tm, issue, 0, unroll=ROW_DMA_UNROLL)

    def drain(t, c):
        _row_copy(ys_ref, 0, b1, 0, sem).wait()
        _row_copy(ys_ref, 0, b2, 0, sem).wait()
        return c

    lax.fori_loop(0, tm, drain, 0, unroll=ROW_DMA_UNROLL)

    route = route_ref[...]
    g1 = route[:, ROUTE_G1:ROUTE_G1 + 1]
    g2 = route[:, ROUTE_G2:ROUTE_G2 + 1]
    out = h_ref[...] + (g1 * b1[...] + g2 * b2[...])
    if final_norm:
        out = _rms(out, gn_ref[...])
    o_ref[...] = out


def _combine(h, route, dest1, dest2, ys, g_final, final_norm, tm=512):
    n, d = h.shape
    idx = pl.BlockSpec((None, 1, tm), lambda i: (i, 0, 0), memory_space=pltpu.SMEM)
    row = pl.BlockSpec((tm, d), lambda i: (i, 0))
    return pl.pallas_call(
        functools.partial(_combine_kernel, tm=tm, final_norm=final_norm),
        grid=(n // tm,),
        in_specs=[idx, idx, row, pl.BlockSpec((tm, LANES), lambda i: (i, 0)),
                  pl.BlockSpec((1, d), lambda i: (0, 0)), pl.BlockSpec(memory_space=pl.ANY)],
        out_specs=row,
        out_shape=jax.ShapeDtypeStruct((n, d), F32),
        scratch_shapes=[pltpu.VMEM((tm, d), F32), pltpu.VMEM((tm, d), F32), pltpu.SemaphoreType.DMA(())],
        compiler_params=_params("arbitrary"),
        name="expert_combine_residual",
    )(dest1.reshape(n // tm, 1, tm), dest2.reshape(n // tm, 1, tm), h, route, g_final, ys)


def _moe_layer(h, route, counts, xn, layer, wg, wu, wd, g_final, final_norm, tm=1024):
    n, d = h.shape
    n_tiles_max = 2 * n // tm + N_EXPERTS
    counts = counts[0, :N_EXPERTS].astype(jnp.int32)
    padded = (counts + tm - 1) // tm * tm
    ends = jnp.cumsum(padded)
    offs = ends - padded
    col = lambda k: route[:, k].astype(jnp.int32)
    dest1 = offs[col(ROUTE_E1)] + col(ROUTE_R1)
    dest2 = offs[col(ROUTE_E2)] + col(ROUTE_R2)
    tile_start = jnp.arange(n_tiles_max, dtype=jnp.int32) * tm
    last_start = jnp.minimum(tile_start, ends[-1] - tm)
    tile_expert = jnp.sum((last_start[:, None] >= ends[None, :]).astype(jnp.int32), axis=1)
    tile_rows = jnp.clip((offs + counts)[tile_expert] - tile_start, 0, tm).astype(jnp.int32)
    xs = _dispatch(xn, dest1, dest2, n_tiles_max * tm)
    ys = _moe(xs, tile_expert, tile_rows, layer, wg, wu, wd, tm)
    return _combine(h, route, dest1, dest2, ys, g_final, final_norm)


def kernel(x, positions, norm_mix_even, w_in, conv_w, w_out, norm_ffn_even, w_gate, w_up, w_down,
           norm_mix_odd, pool_w, pool_scale, norm_ffn_odd, router_w, exp_w_gate, exp_w_up, exp_w_down,
           final_norm):
    batch, seq, d = x.shape
    n = batch * seq
    depth = norm_mix_even.shape[0] + norm_mix_odd.shape[0]
    assert depth % 2 == 0, "the final norm is fused into the last (odd) layer's combine"
    h = x.reshape(n, d)
    cos, sin = _rope_tables(positions)
    g_final = final_norm[None, :]
    for layer in range(depth):
        i = layer // 2
        if layer % 2 == 0:
            g_mix, w_in_b = norm_mix_even[i][None, :], w_in[i].astype(BF16)
            n_rot = 2 * ATTN_WIDTH
            qk = _inproj(h, g_mix, w_in_b, cos, sin, 0, n_rot, True)
            rest = _inproj(h, g_mix, w_in_b, cos, sin, n_rot, w_in_b.shape[1] - n_rot, False)
            attn = _attention(qk.reshape(batch, seq, -1), rest.reshape(batch, seq, -1), batch, seq)
            h = _outproj(h, attn.reshape(n, ATTN_WIDTH), rest, conv_w[i], w_out[i].astype(BF16), seq)
            h = _ffn(h, norm_ffn_even[i][None, :], w_gate[i].astype(BF16), w_up[i].astype(BF16),
                     w_down[i].astype(BF16))
        else:
            h, xn, route, counts = _pool_router(h, norm_mix_odd[i][None, :], pool_w[i].astype(BF16),
                                                pool_scale[i][None, :], norm_ffn_odd[i][None, :],
                                                router_w[i], seq)
            h = _moe_layer(h, route, counts, xn, i, exp_w_gate, exp_w_up, exp_w_down, g_final,
                           layer == depth - 1)
    return h.reshape(batch, seq, d)
```

```python
import functools
import math

import jax
import jax.numpy as jnp
from jax import lax
from jax.experimental import pallas as pl
from jax.experimental.pallas import tpu as pltpu

F32 = jnp.float32
BF16 = jnp.bfloat16

HEAD_DIM = 128
N_HEADS = 12
ATTN_WIDTH = N_HEADS * HEAD_DIM
CONV_WIDTH = 512
CONV_K = 3
DILATIONS = ((128, 1), (512, 4), (2048, 16))
KEYS_BACK = 128
ROPE_THETA = 10000.0
POOL_WINDOWS = (2, 4, 8, 16)
N_EXPERTS = 8
RMS_EPS = 1e-5
NEG_INF = -1e30
LANES = 128
HALO = 16
VMEM_LIMIT = 56 * 1024 * 1024
ATTN_TILE_BLOCKS = 4
ATTN_BASE_DILATION = 4
ATTN_INTERLEAVE = 16
ROW_DMA_UNROLL = 8

def _params(*sem):
    return pltpu.CompilerParams(dimension_semantics=sem, vmem_limit_bytes=VMEM_LIMIT)


def _rms(x, g):
    return x * lax.rsqrt(jnp.mean(x * x, axis=-1, keepdims=True) + RMS_EPS) * g


def _pack_bf16_pairs(x):
    half = x.shape[1] // 2
    bits = lax.bitcast_convert_type(x.astype(BF16).astype(F32), jnp.uint32)
    return (bits[:, :half] >> 16) | (bits[:, half:] & jnp.uint32(0xFFFF0000))


def _unpack_bf16_pairs(words):
    lo = lax.bitcast_convert_type(words << 16, F32).astype(BF16)
    hi = lax.bitcast_convert_type(words & jnp.uint32(0xFFFF0000), F32).astype(BF16)
    return lo, hi


def _rope_kernel(pos_ref, freq_ref, sign_ref, cos_ref, sin_ref):
    ang = pos_ref[...].astype(F32) * freq_ref[...]
    cos_ref[...] = jnp.cos(ang)
    sin_ref[...] = jnp.sin(ang) * sign_ref[...]


def _rope_tables(positions, tm=1024):
    n = positions.size
    half = HEAD_DIM // 2
    inv_freq = jnp.power(ROPE_THETA, -jnp.arange(half, dtype=F32) / half)
    freq = jnp.concatenate([inv_freq, inv_freq])[None, :]
    sign = jnp.concatenate([-jnp.ones((half,), F32), jnp.ones((half,), F32)])[None, :]
    row = pl.BlockSpec((tm, HEAD_DIM), lambda i: (i, 0))
    const = pl.BlockSpec((1, HEAD_DIM), lambda i: (0, 0))
    return pl.pallas_call(
        _rope_kernel,
        grid=(n // tm,),
        in_specs=[pl.BlockSpec((tm, 1), lambda i: (i, 0)), const, const],
        out_specs=[row, row],
        out_shape=[jax.ShapeDtypeStruct((n, HEAD_DIM), F32)] * 2,
        compiler_params=_params("parallel"),
        name="rope_tables",
    )(positions.reshape(n, 1), freq, sign)


def _inproj_kernel(h_ref, g_ref, w_ref, cos_ref, sin_ref, o_ref, xn_ref, *, rotary, tn):
    j = pl.program_id(1)

    @pl.when(j == 0)
    def _():
        xn_ref[...] = _rms(h_ref[...], g_ref[...]).astype(BF16)

    y = jnp.dot(xn_ref[...], w_ref[...], preferred_element_type=F32)
    if rotary:
        cos = cos_ref[...]
        sin = sin_ref[...]
        for c in range(tn // HEAD_DIM):
            t = y[:, c * HEAD_DIM:(c + 1) * HEAD_DIM]
            rot = t * cos + pltpu.roll(t, HEAD_DIM // 2, axis=1) * sin
            o_ref[:, c * HEAD_DIM:(c + 1) * HEAD_DIM] = rot.astype(BF16)
    else:
        o_ref[...] = y.astype(BF16)


def _inproj(h, g, w, cos, sin, col0, width, rotary, tm=1024, tn=1536):
    n, d = h.shape
    assert width % tn == 0 and col0 % tn == 0 and n % tm == 0
    jb = col0 // tn
    return pl.pallas_call(
        functools.partial(_inproj_kernel, rotary=rotary, tn=tn),
        grid=(n // tm, width // tn),
        in_specs=[
            pl.BlockSpec((tm, d), lambda i, j: (i, 0)),
            pl.BlockSpec((1, d), lambda i, j: (0, 0)),
            pl.BlockSpec((d, tn), lambda i, j: (0, jb + j)),
            pl.BlockSpec((tm, HEAD_DIM), lambda i, j: (i, 0)),
            pl.BlockSpec((tm, HEAD_DIM), lambda i, j: (i, 0)),
        ],
        out_specs=pl.BlockSpec((tm, tn), lambda i, j: (i, j)),
        out_shape=jax.ShapeDtypeStruct((n, width), BF16),
        scratch_shapes=[pltpu.VMEM((tm, d), BF16)],
        compiler_params=_params("parallel", "arbitrary"),
        name="norm_inproj_rotary" if rotary else "norm_inproj_plain",
    )(h, g, w, cos, sin)


def _attn_kernel(q_ref, k_ref, v_ref, o_ref, nat, q16, k16, v16, acc, m_s, l_s, *, seq):
    blk = KEYS_BACK
    n_cls = ATTN_BASE_DILATION
    cls_len = seq // n_cls
    scale = HEAD_DIM ** -0.5 * math.log2(math.e)
    last = len(DILATIONS) - 1
    nt = (((1,), (1,)), ((), ()))
    u = ATTN_INTERLEAVE

    def cls_rows(r):
        return pl.ds(pl.multiple_of(r * cls_len, cls_len), cls_len)

    for src_ref, dst in ((q_ref, q16), (k_ref, k16), (v_ref, v16)):
        nat[...] = src_ref[...].astype(F32)

        def split(r, c, dst=dst):
            dst[cls_rows(r), :] = nat[pl.ds(r, cls_len, stride=n_cls), :]
            return c

        lax.fori_loop(0, n_cls, split, 0)

    for gi, (window, d) in enumerate(DILATIONS):
        assert window // d == blk and (n_cls % d == 0 or d % n_cls == 0)
        n_blk = seq // d // blk
        nb = min(ATTN_TILE_BLOCKS, n_blk)
        tiles_per_class = n_blk // nb
        pieces = max(n_cls // d, 1)
        stride = max(d // n_cls, 1)
        plen = blk // pieces
        idx0 = lax.broadcasted_iota(jnp.int32, (blk, blk), 0)
        idx1 = lax.broadcasted_iota(jnp.int32, (blk, blk), 1)
        pos0 = pieces * (idx0 % plen) + idx0 // plen
        pos1 = pieces * (idx1 % plen) + idx1 // plen
        mask_cur = pos1 <= pos0
        mask_prev = pos1 >= pos0
        mask_both = jnp.concatenate([mask_prev, mask_cur], axis=1)

        def piece_rows(rc, b, c, d=d, plen=plen, stride=stride):
            if stride > 1:
                return pl.ds((rc % n_cls) * cls_len + rc // n_cls + stride * blk * b, blk, stride=stride)
            return pl.ds(pl.multiple_of((d * c + rc) * cls_len + plen * b, 8), plen)

        def load_block(ref, rc, b, pieces=pieces, piece_rows=piece_rows):
            return jnp.concatenate([ref[piece_rows(rc, b, c), :] for c in range(pieces)], axis=0)

        def store_block(ref, rc, b, val, pieces=pieces, plen=plen, piece_rows=piece_rows):
            for c in range(pieces):
                ref[piece_rows(rc, b, c), :] = val[c * plen:(c + 1) * plen, :]

        def load_tile(rc, b0, first, gi=gi, nb=nb, load_block=load_block):
            ko = 0 if first else 1
            t = dict(rc=rc, b0=b0, ko=ko,
                     q=[load_block(q16, rc, b0 + b).astype(BF16) for b in range(nb)],
                     k=[load_block(k16, rc, b0 + b - ko).astype(BF16) for b in range(nb + ko)],
                     v=[load_block(v16, rc, b0 + b - ko).astype(BF16) for b in range(nb + ko)])
            if gi > 0:
                t.update(m=[load_block(m_s, rc, b0 + b) for b in range(nb)],
                         l=[load_block(l_s, rc, b0 + b) for b in range(nb)],
                         a=[load_block(acc, rc, b0 + b) for b in range(nb)])
            return t

        def compute_tile(t, gi=gi, nb=nb, mask_cur=mask_cur, mask_both=mask_both):
            out = []
            for b in range(nb):
                own = b + t["ko"]
                if own > 0:
                    k = jnp.concatenate([t["k"][own - 1], t["k"][own]], axis=0)
                    v = jnp.concatenate([t["v"][own - 1], t["v"][own]], axis=0)
                    mask = mask_both
                else:
                    k, v, mask = t["k"][own], t["v"][own], mask_cur
                s = lax.dot_general(t["q"][b], k, nt, preferred_element_type=F32)
                s = jnp.where(mask, s * scale, NEG_INF)
                m_blk = jnp.max(s, axis=-1, keepdims=True)
                if gi == 0:
                    m_new = jnp.broadcast_to(m_blk, (blk, blk))
                else:
                    m_old = t["m"][b]
                    m_new = jnp.maximum(m_old, m_blk)
                p_parts = [jnp.exp2(s[:, c * blk:(c + 1) * blk] - m_new) for c in range(s.shape[1] // blk)]
                p_sum = p_parts[0] if len(p_parts) == 1 else p_parts[0] + p_parts[1]
                l_new = jnp.broadcast_to(jnp.sum(p_sum, axis=-1, keepdims=True), (blk, blk))
                p = jnp.concatenate([x.astype(BF16) for x in p_parts], axis=1)
                pv = jnp.dot(p, v, preferred_element_type=F32)
                if gi > 0:
                    alpha = jnp.exp2(m_old - m_new)
                    l_new = alpha * t["l"][b] + l_new
                    pv = alpha * t["a"][b] + pv
                if gi == last:
                    pv = pv / l_new
                out.append((pv, m_new, l_new))
            return out

        def run_tiles(tiles, gi=gi, nb=nb, load_tile=load_tile, compute_tile=compute_tile,
                      store_block=store_block):
            loaded = [load_tile(rc, b0, first) for rc, b0, first in tiles]
            results = [compute_tile(t) for t in loaded]
            for t, res in zip(loaded, results):
                for b, (pv, m_new, l_new) in enumerate(res):
                    store_block(acc, t["rc"], t["b0"] + b, pv)
                    if gi < last:
                        store_block(m_s, t["rc"], t["b0"] + b, m_new)
                        store_block(l_s, t["rc"], t["b0"] + b, l_new)

        ug = min(u, d * tiles_per_class)
        if tiles_per_class <= ug:
            classes = ug // tiles_per_class
            assert ug % tiles_per_class == 0 and d % classes == 0

            def body(it, c, run_tiles=run_tiles, nb=nb, classes=classes, tpc=tiles_per_class):
                run_tiles([(it * classes + cl, k * nb, k == 0) for cl in range(classes) for k in range(tpc)])
                return c

            if d == classes:
                body(0, 0)
            else:
                lax.fori_loop(0, d // classes, body, 0)
        else:
            assert d == 1 and tiles_per_class % ug == 0
            run_tiles([(0, k * nb, k == 0) for k in range(ug)])

            def body(it, c, run_tiles=run_tiles, nb=nb, ug=ug):
                run_tiles([(0, (it * ug + k) * nb, False) for k in range(ug)])
                return c

            lax.fori_loop(1, tiles_per_class // ug, body, 0)

    def merge(r, c):
        nat[pl.ds(r, cls_len, stride=n_cls), :] = acc[cls_rows(r), :]
        return c

    lax.fori_loop(0, n_cls, merge, 0)
    o_ref[...] = nat[...].astype(BF16)


def _attention(qk, rest, batch, seq):
    assert all(seq % (d * KEYS_BACK) == 0 for _, d in DILATIONS), "every residue class is whole blocks"
    blockspec = lambda off: pl.BlockSpec((None, seq, HEAD_DIM), lambda b, h: (b, 0, off + h))
    return pl.pallas_call(
        functools.partial(_attn_kernel, seq=seq),
        grid=(batch, N_HEADS),
        in_specs=[blockspec(0), blockspec(N_HEADS), blockspec(0)],
        out_specs=blockspec(0),
        out_shape=jax.ShapeDtypeStruct((batch, seq, ATTN_WIDTH), BF16),
        scratch_shapes=[pltpu.VMEM((seq, HEAD_DIM), F32)] * 7,
        compiler_params=_params("parallel", "parallel"),
        name="dilated_attention",
    )(qk, qk, rest)


def _outproj_kernel(h_ref, a_ref, gb_ref, gc_ref, hv_ref, gch_ref, hvh_ref, cw_ref, wo_ref, o_ref,
                    ext, *, tm, seq):
    i = pl.program_id(0)
    first = (i * tm) % seq == 0
    u = gc_ref[...].astype(F32) * hv_ref[...].astype(F32)
    u_halo = gch_ref[...].astype(F32) * hvh_ref[...].astype(F32)
    ext[0:HALO, :] = jnp.where(first, 0.0, u_halo)
    ext[HALO:HALO + tm, :] = u
    w = cw_ref[...]
    conv = w[CONV_K - 1:CONV_K, :] * u
    for back in range(1, CONV_K):
        conv = conv + w[CONV_K - 1 - back:CONV_K - back, :] * ext[pl.ds(HALO - back, tm), :]
    c = (gb_ref[...].astype(F32) * conv).astype(BF16)
    y = jnp.dot(a_ref[...], wo_ref[0:ATTN_WIDTH, :], preferred_element_type=F32)
    y = y + jnp.dot(c, wo_ref[ATTN_WIDTH:, :], preferred_element_type=F32)
    o_ref[...] = h_ref[...] + y


def _outproj(h, attn, proj, conv_w, w_out, seq, tm=512):
    n, d = h.shape
    assert seq % tm == 0 and tm % HALO == 0, "row tiles must not straddle sequences"
    cb = ATTN_WIDTH // CONV_WIDTH
    hb = tm // HALO
    cur = lambda k: pl.BlockSpec((tm, CONV_WIDTH), lambda i: (i, cb + k))
    halo = lambda k: pl.BlockSpec((HALO, CONV_WIDTH), lambda i: (jnp.maximum(i * hb - 1, 0), cb + k))
    return pl.pallas_call(
        functools.partial(_outproj_kernel, tm=tm, seq=seq),
        grid=(n // tm,),
        in_specs=[
            pl.BlockSpec((tm, d), lambda i: (i, 0)),
            pl.BlockSpec((tm, ATTN_WIDTH), lambda i: (i, 0)),
            cur(0), cur(1), cur(2), halo(1), halo(2),
            pl.BlockSpec((CONV_K, CONV_WIDTH), lambda i: (0, 0)),
            pl.BlockSpec((d, d), lambda i: (0, 0)),
        ],
        out_specs=pl.BlockSpec((tm, d), lambda i: (i, 0)),
        out_shape=jax.ShapeDtypeStruct((n, d), F32),
        scratch_shapes=[pltpu.VMEM((HALO + tm, CONV_WIDTH), F32)],
        compiler_params=_params("parallel"),
        name="conv_outproj_residual",
    )(h, attn, proj, proj, proj, proj, proj, conv_w, w_out)


def _ffn_kernel(h_ref, g_ref, wg_ref, wu_ref, wd_ref, o_ref, xn_ref):
    j = pl.program_id(1)

    @pl.when(j == 0)
    def _():
        h = h_ref[...]
        xn_ref[...] = _rms(h, g_ref[...]).astype(BF16)
        o_ref[...] = h

    x = xn_ref[...]
    gate = jnp.dot(x, wg_ref[...], preferred_element_type=F32)
    up = jnp.dot(x, wu_ref[...], preferred_element_type=F32)
    act = (gate * jax.nn.sigmoid(gate) * up).astype(BF16)
    o_ref[...] += jnp.dot(act, wd_ref[...], preferred_element_type=F32)


def _ffn(h, g, wg, wu, wd, tm=1024, tf=512):
    n, d = h.shape
    f = wg.shape[1]
    assert n % tm == 0 and f % tf == 0
    return pl.pallas_call(
        _ffn_kernel,
        grid=(n // tm, f // tf),
        in_specs=[
            pl.BlockSpec((tm, d), lambda i, j: (i, 0)),
            pl.BlockSpec((1, d), lambda i, j: (0, 0)),
            pl.BlockSpec((d, tf), lambda i, j: (0, j)),
            pl.BlockSpec((d, tf), lambda i, j: (0, j)),
            pl.BlockSpec((tf, d), lambda i, j: (j, 0)),
        ],
        out_specs=pl.BlockSpec((tm, d), lambda i, j: (i, 0)),
        out_shape=jax.ShapeDtypeStruct((n, d), F32),
        scratch_shapes=[pltpu.VMEM((tm, d), BF16)],
        compiler_params=_params("parallel", "arbitrary"),
        name="norm_swiglu_residual",
    )(h, g, wg, wu, wd)


ROUTE_E1, ROUTE_E2, ROUTE_R1, ROUTE_R2, ROUTE_G1, ROUTE_G2 = range(6)


def _pool_router_kernel(h_ref, hh_ref, gm_ref, pw_ref, ps_ref, gf_ref, rw_ref,
                        hn_ref, xn_ref, route_ref, cnt_ref, ext, run, *, tm, seq):
    i = pl.program_id(0)

    @pl.when(i == 0)
    def _():
        run[...] = jnp.zeros_like(run)

    pos0 = (i * tm) % seq
    h = h_ref[...]
    gm = gm_ref[...]
    xn = _rms(h, gm)
    ext[0:HALO, :] = jnp.where(pos0 == 0, 0.0, _rms(hh_ref[...], gm))
    ext[HALO:HALO + tm, :] = xn
    t1 = (pos0 + 1 + lax.broadcasted_iota(jnp.int32, (tm, 1), 0)).astype(F32)
    group = h.shape[1] // len(POOL_WINDOWS)
    outs = []
    for gi, w in enumerate(POOL_WINDOWS):
        cols = slice(gi * group, (gi + 1) * group)
        x_g = xn[:, cols]
        assert w & (w - 1) == 0 and w <= HALO
        s = ext[:, cols]
        shift = 1
        while shift < w:
            s = s + pltpu.roll(s, shift, axis=0)
            shift *= 2
        s = s[HALO:, :]
        mixed = s / jnp.minimum(t1, float(w)) - x_g
        outs.append(jnp.dot(mixed.astype(BF16), pw_ref[gi], preferred_element_type=F32))
    hn = h + jnp.concatenate(outs, axis=1) * ps_ref[...]
    hn_ref[...] = hn

    xn2 = _rms(hn, gf_ref[...])
    xn_ref[...] = _pack_bf16_pairs(xn2)
    logits = jnp.dot(xn2, rw_ref[...], preferred_element_type=F32, precision=lax.Precision.HIGHEST)
    lane = lax.broadcasted_iota(jnp.int32, (tm, LANES), 1)
    logits = jnp.where(lane < N_EXPERTS, logits, -jnp.inf)
    m1 = jnp.max(logits, axis=-1, keepdims=True)
    e1 = jnp.min(jnp.where(logits == m1, lane, LANES), axis=-1, keepdims=True)
    rest = jnp.where(lane == e1, -jnp.inf, logits)
    m2 = jnp.max(rest, axis=-1, keepdims=True)
    e2 = jnp.min(jnp.where(rest == m2, lane, LANES), axis=-1, keepdims=True)
    ex = jnp.exp(m2 - m1)
    g1 = 1.0 / (1.0 + ex)
    g2 = ex / (1.0 + ex)

    hit1 = lane == e1
    hit2 = lane == e2
    onehot = jnp.where(hit1 | hit2, 1.0, 0.0)
    tri = (lax.broadcasted_iota(jnp.int32, (tm, tm), 1) < lax.broadcasted_iota(jnp.int32, (tm, tm), 0))
    before = jnp.dot(tri.astype(BF16), onehot.astype(BF16), preferred_element_type=F32) + run[...]
    r1 = jnp.sum(jnp.where(hit1, before, 0.0), axis=-1, keepdims=True)
    r2 = jnp.sum(jnp.where(hit2, before, 0.0), axis=-1, keepdims=True)
    run[...] = run[...] + jnp.sum(onehot, axis=0, keepdims=True)
    cnt_ref[...] = run[...]

    route = jnp.zeros((tm, LANES), F32)
    for slot, val in ((ROUTE_E1, e1.astype(F32)), (ROUTE_E2, e2.astype(F32)), (ROUTE_R1, r1),
                      (ROUTE_R2, r2), (ROUTE_G1, g1), (ROUTE_G2, g2)):
        route = jnp.where(lane == slot, val, route)
    route_ref[...] = route


def _pool_router(h, g_mix, pool_w, pool_scale, g_ffn, router_w, seq, tm=512):
    n, d = h.shape
    assert seq % tm == 0 and tm % HALO == 0, "row tiles must not straddle sequences"
    hb = tm // HALO
    row = pl.BlockSpec((tm, d), lambda i: (i, 0))
    vec = pl.BlockSpec((1, d), lambda i: (0, 0))
    rw = jnp.zeros((d, LANES), F32).at[:, :N_EXPERTS].set(router_w)
    return pl.pallas_call(
        functools.partial(_pool_router_kernel, tm=tm, seq=seq),
        grid=(n // tm,),
        in_specs=[
            row,
            pl.BlockSpec((HALO, d), lambda i: (jnp.maximum(i * hb - 1, 0), 0)),
            vec,
            pl.BlockSpec(pool_w.shape, lambda i: (0, 0, 0)),
            vec, vec,
            pl.BlockSpec((d, LANES), lambda i: (0, 0)),
        ],
        out_specs=[row, pl.BlockSpec((tm, d // 2), lambda i: (i, 0)),
                   pl.BlockSpec((tm, LANES), lambda i: (i, 0)), pl.BlockSpec((1, LANES), lambda i: (0, 0))],
        out_shape=[jax.ShapeDtypeStruct((n, d), F32), jax.ShapeDtypeStruct((n, d // 2), jnp.uint32),
                   jax.ShapeDtypeStruct((n, LANES), F32), jax.ShapeDtypeStruct((1, LANES), F32)],
        scratch_shapes=[pltpu.VMEM((HALO + tm, d), F32), pltpu.VMEM((1, LANES), F32)],
        compiler_params=_params("arbitrary"),
        name="pool_residual_router",
    )(h, h, g_mix, pool_w, pool_scale, g_ffn, rw)


def _row_copy(src_ref, src_row, dst_ref, dst_row, sem):
    return pltpu.make_async_copy(src_ref.at[pl.ds(src_row, 1)], dst_ref.at[pl.ds(dst_row, 1)], sem)


def _dispatch_kernel(d1_ref, d2_ref, x_ref, xs_in_ref, xs_ref, sem, *, tm):
    del xs_in_ref

    def issue(t, c):
        _row_copy(x_ref, t, xs_ref, d1_ref[0, t], sem).start()
        _row_copy(x_ref, t, xs_ref, d2_ref[0, t], sem).start()
        return c

    lax.fori_loop(0, tm, issue, 0, unroll=ROW_DMA_UNROLL)

    def drain(t, c):
        _row_copy(x_ref, 0, xs_ref, 0, sem).wait()
        _row_copy(x_ref, 0, xs_ref, 0, sem).wait()
        return c

    lax.fori_loop(0, tm, drain, 0, unroll=ROW_DMA_UNROLL)


def _dispatch(xn, dest1, dest2, n_rows, tm=512):
    n, w = xn.shape
    idx = pl.BlockSpec((None, 1, tm), lambda i: (i, 0, 0), memory_space=pltpu.SMEM)
    return pl.pallas_call(
        functools.partial(_dispatch_kernel, tm=tm),
        grid=(n // tm,),
        in_specs=[idx, idx, pl.BlockSpec((tm, w), lambda i: (i, 0)), pl.BlockSpec(memory_space=pl.ANY)],
        out_specs=pl.BlockSpec(memory_space=pl.ANY),
        out_shape=jax.ShapeDtypeStruct((n_rows, w), xn.dtype),
        scratch_shapes=[pltpu.SemaphoreType.DMA(())],
        input_output_aliases={3: 0},
        compiler_params=_params("arbitrary"),
        name="expert_dispatch",
    )(dest1.reshape(n // tm, 1, tm), dest2.reshape(n // tm, 1, tm), xn, jnp.zeros((n_rows, w), xn.dtype))


def _moe_kernel(te_ref, tr_ref, x_ref, wg_ref, wu_ref, wd_ref, o_ref, xb_ref):
    del te_ref
    i = pl.program_id(0)
    j = pl.program_id(1)
    rows = tr_ref[i]
    half = x_ref.shape[1]

    @pl.when(j == 0)
    def _():
        o_ref[...] = jnp.zeros_like(o_ref)

    @pl.when((rows > 0) & (j == 0))
    def _():
        lo, hi = _unpack_bf16_pairs(x_ref[...])
        xb_ref[:, :half] = lo
        xb_ref[:, half:] = hi

    def swiglu_top_rows(n_rows):
        r = pl.ds(0, n_rows)
        x = xb_ref[r, :]
        gate = jnp.dot(x, wg_ref[...].astype(BF16), preferred_element_type=F32)
        up = jnp.dot(x, wu_ref[...].astype(BF16), preferred_element_type=F32)
        act = (gate * jax.nn.sigmoid(gate) * up).astype(BF16)
        o_ref[r, :] += jnp.dot(act, wd_ref[...].astype(BF16), preferred_element_type=F32)

    tm = o_ref.shape[0]
    pl.when(rows > tm // 2)(lambda: swiglu_top_rows(tm))
    pl.when((rows > tm // 4) & (rows <= tm // 2))(lambda: swiglu_top_rows(tm // 2))
    pl.when((rows > 0) & (rows <= tm // 4))(lambda: swiglu_top_rows(tm // 4))


def _moe(xs, tile_expert, tile_rows, layer, wg, wu, wd, tm, tf=256):
    p, half = xs.shape
    d = 2 * half
    f = wg.shape[3]
    nj = f // tf
    assert p % tm == 0 and f % tf == 0

    def col(i, j, tr):
        return jnp.where(tr[i] > 0, j, nj - 1)

    grid_spec = pltpu.PrefetchScalarGridSpec(
        num_scalar_prefetch=2,
        grid=(p // tm, nj),
        in_specs=[
            pl.BlockSpec((tm, half), lambda i, j, te, tr: (jnp.where(tr[i] > 0, i, 0), 0)),
            pl.BlockSpec((None, None, d, tf), lambda i, j, te, tr: (layer, te[i], 0, col(i, j, tr))),
            pl.BlockSpec((None, None, d, tf), lambda i, j, te, tr: (layer, te[i], 0, col(i, j, tr))),
            pl.BlockSpec((None, None, tf, d), lambda i, j, te, tr: (layer, te[i], col(i, j, tr), 0)),
        ],
        out_specs=pl.BlockSpec((tm, d), lambda i, j, te, tr: (i, 0)),
        scratch_shapes=[pltpu.VMEM((tm, d), BF16)],
    )
    return pl.pallas_call(
        _moe_kernel,
        grid_spec=grid_spec,
        out_shape=jax.ShapeDtypeStruct((p, d), F32),
        compiler_params=_params("arbitrary", "arbitrary"),
        name="grouped_expert_swiglu",
    )(tile_expert, tile_rows, xs, wg, wu, wd)


def _combine_kernel(d1_ref, d2_ref, n1_ref, n2_ref, h_ref, route_ref, gn_ref, ys_ref, o_ref, b1, b2, sem,
                    *, tm, final_norm):
    i = pl.program_id(0)
    slot = i % 2

    def start_gather(i1_ref, i2_ref, s):
        def issue(t, c):
            _row_copy(ys_ref, i1_ref[0, t], b1.at[s], t, sem.at[s]).start()
            _row_copy(ys_ref, i2_ref[0, t], b2.at[s], t, sem.at[s]).start()
            return c

        lax.fori_loop(0, tm, issue, 0, unroll=ROW_DMA_UNROLL)

    pl.when(i == 0)(lambda: start_gather(d1_ref, d2_ref, 0))
    pl.when(i + 1 < pl.num_programs(0))(lambda: start_gather(n1_ref, n2_ref, 1 - slot))

    def drain(t, c):
        _row_copy(ys_ref, 0, b1.at[slot], 0, sem.at[slot]).wait()
        _row_copy(ys_ref, 0, b2.at[slot], 0, sem.at[slot]).wait()
        return c

    lax.fori_loop(0, tm, drain, 0, unroll=ROW_DMA_UNROLL)

    route = route_ref[...]
    g1 = route[:, ROUTE_G1:ROUTE_G1 + 1]
    g2 = route[:, ROUTE_G2:ROUTE_G2 + 1]
    out = h_ref[...] + (g1 * b1[slot] + g2 * b2[slot])
    if final_norm:
        out = _rms(out, gn_ref[...])
    o_ref[...] = out


def _combine(h, route, dest1, dest2, ys, g_final, final_norm, tm=512):
    n, d = h.shape
    assert n % tm == 0
    steps = n // tm
    idx = pl.BlockSpec((None, 1, tm), lambda i: (i, 0, 0), memory_space=pltpu.SMEM)
    nxt = pl.BlockSpec((None, 1, tm), lambda i: (jnp.minimum(i + 1, steps - 1), 0, 0), memory_space=pltpu.SMEM)
    row = pl.BlockSpec((tm, d), lambda i: (i, 0))
    d1 = dest1.reshape(steps, 1, tm)
    d2 = dest2.reshape(steps, 1, tm)
    return pl.pallas_call(
        functools.partial(_combine_kernel, tm=tm, final_norm=final_norm),
        grid=(steps,),
        in_specs=[idx, idx, nxt, nxt, row, pl.BlockSpec((tm, LANES), lambda i: (i, 0)),
                  pl.BlockSpec((1, d), lambda i: (0, 0)), pl.BlockSpec(memory_space=pl.ANY)],
        out_specs=row,
        out_shape=jax.ShapeDtypeStruct((n, d), F32),
        scratch_shapes=[pltpu.VMEM((2, tm, d), F32), pltpu.VMEM((2, tm, d), F32), pltpu.SemaphoreType.DMA((2,))],
        compiler_params=_params("arbitrary"),
        name="expert_combine_residual",
    )(d1, d2, d1, d2, h, route, g_final, ys)


def _moe_layer(h, route, counts, xn, layer, wg, wu, wd, g_final, final_norm, tm=1024):
    n, d = h.shape
    n_tiles_max = 2 * n // tm + N_EXPERTS
    counts = counts[0, :N_EXPERTS].astype(jnp.int32)
    padded = (counts + tm - 1) // tm * tm
    ends = jnp.cumsum(padded)
    offs = ends - padded
    col = lambda k: route[:, k].astype(jnp.int32)
    dest1 = offs[col(ROUTE_E1)] + col(ROUTE_R1)
    dest2 = offs[col(ROUTE_E2)] + col(ROUTE_R2)
    tile_start = jnp.arange(n_tiles_max, dtype=jnp.int32) * tm
    last_start = jnp.minimum(tile_start, ends[-1] - tm)
    tile_expert = jnp.sum((last_start[:, None] >= ends[None, :]).astype(jnp.int32), axis=1)
    tile_rows = jnp.clip((offs + counts)[tile_expert] - tile_start, 0, tm).astype(jnp.int32)
    xs = _dispatch(xn, dest1, dest2, n_tiles_max * tm)
    ys = _moe(xs, tile_expert, tile_rows, layer, wg, wu, wd, tm)
    return _combine(h, route, dest1, dest2, ys, g_final, final_norm)


def kernel(x, positions, norm_mix_even, w_in, conv_w, w_out, norm_ffn_even, w_gate, w_up, w_down,
           norm_mix_odd, pool_w, pool_scale, norm_ffn_odd, router_w, exp_w_gate, exp_w_up, exp_w_down,
           final_norm):
    batch, seq, d = x.shape
    n = batch * seq
    depth = norm_mix_even.shape[0] + norm_mix_odd.shape[0]
    assert depth % 2 == 0, "the final norm is fused into the last (odd) layer's combine"
    h = x.reshape(n, d)
    cos, sin = _rope_tables(positions)
    g_final = final_norm[None, :]
    for layer in range(depth):
        i = layer // 2
        if layer % 2 == 0:
            g_mix, w_in_b = norm_mix_even[i][None, :], w_in[i].astype(BF16)
            n_rot = 2 * ATTN_WIDTH
            qk = _inproj(h, g_mix, w_in_b, cos, sin, 0, n_rot, True)
            rest = _inproj(h, g_mix, w_in_b, cos, sin, n_rot, w_in_b.shape[1] - n_rot, False)
            attn = _attention(qk.reshape(batch, seq, -1), rest.reshape(batch, seq, -1), batch, seq)
            h = _outproj(h, attn.reshape(n, ATTN_WIDTH), rest, conv_w[i], w_out[i].astype(BF16), seq)
            h = _ffn(h, norm_ffn_even[i][None, :], w_gate[i].astype(BF16), w_up[i].astype(BF16),
                     w_down[i].astype(BF16))
        else:
            h, xn, route, counts = _pool_router(h, norm_mix_odd[i][None, :], pool_w[i].astype(BF16),
                                                pool_scale[i][None, :], norm_ffn_odd[i][None, :],
                                                router_w[i], seq)
            h = _moe_layer(h, route, counts, xn, i, exp_w_gate, exp_w_up, exp_w_down, g_final,
                           layer == depth - 1)
    return h.reshape(batch, seq, d)
```

```python
import functools
import math

import jax
import jax.numpy as jnp
from jax import lax
from jax.experimental import pallas as pl
from jax.experimental.pallas import tpu as pltpu

F32 = jnp.float32
BF16 = jnp.bfloat16

HEAD_DIM = 128
N_HEADS = 12
ATTN_WIDTH = N_HEADS * HEAD_DIM
CONV_WIDTH = 512
CONV_K = 3
DILATIONS = ((128, 1), (512, 4), (2048, 16))
KEYS_BACK = 128
ROPE_THETA = 10000.0
POOL_WINDOWS = (2, 4, 8, 16)
N_EXPERTS = 8
RMS_EPS = 1e-5
NEG_INF = -1e30
LANES = 128
HALO = 16
VMEM_LIMIT = 56 * 1024 * 1024
ATTN_TILE_BLOCKS = 4
ATTN_BASE_DILATION = 4
ATTN_INTERLEAVE = 16
ROW_DMA_UNROLL = 8

def _params(*sem):
    return pltpu.CompilerParams(dimension_semantics=sem, vmem_limit_bytes=VMEM_LIMIT)


def _rms(x, g):
    return x * lax.rsqrt(jnp.mean(x * x, axis=-1, keepdims=True) + RMS_EPS) * g


def _pack_bf16_pairs(x):
    half = x.shape[1] // 2
    bits = lax.bitcast_convert_type(x.astype(BF16).astype(F32), jnp.uint32)
    return (bits[:, :half] >> 16) | (bits[:, half:] & jnp.uint32(0xFFFF0000))


def _unpack_bf16_pairs(words):
    lo = lax.bitcast_convert_type(words << 16, F32).astype(BF16)
    hi = lax.bitcast_convert_type(words & jnp.uint32(0xFFFF0000), F32).astype(BF16)
    return lo, hi


def _rope_kernel(pos_ref, freq_ref, sign_ref, cos_ref, sin_ref):
    ang = pos_ref[...].astype(F32) * freq_ref[...]
    cos_ref[...] = jnp.cos(ang)
    sin_ref[...] = jnp.sin(ang) * sign_ref[...]


def _rope_tables(positions, tm=1024):
    n = positions.size
    half = HEAD_DIM // 2
    inv_freq = jnp.power(ROPE_THETA, -jnp.arange(half, dtype=F32) / half)
    freq = jnp.concatenate([inv_freq, inv_freq])[None, :]
    sign = jnp.concatenate([-jnp.ones((half,), F32), jnp.ones((half,), F32)])[None, :]
    row = pl.BlockSpec((tm, HEAD_DIM), lambda i: (i, 0))
    const = pl.BlockSpec((1, HEAD_DIM), lambda i: (0, 0))
    return pl.pallas_call(
        _rope_kernel,
        grid=(n // tm,),
        in_specs=[pl.BlockSpec((tm, 1), lambda i: (i, 0)), const, const],
        out_specs=[row, row],
        out_shape=[jax.ShapeDtypeStruct((n, HEAD_DIM), F32)] * 2,
        compiler_params=_params("parallel"),
        name="rope_tables",
    )(positions.reshape(n, 1), freq, sign)


def _inproj_kernel(h_ref, g_ref, w_ref, cos_ref, sin_ref, o_ref, xn_ref, *, rotary, tn):
    j = pl.program_id(1)

    @pl.when(j == 0)
    def _():
        xn_ref[...] = _rms(h_ref[...], g_ref[...]).astype(BF16)

    y = jnp.dot(xn_ref[...], w_ref[...], preferred_element_type=F32)
    if rotary:
        cos = cos_ref[...]
        sin = sin_ref[...]
        for c in range(tn // HEAD_DIM):
            t = y[:, c * HEAD_DIM:(c + 1) * HEAD_DIM]
            rot = t * cos + pltpu.roll(t, HEAD_DIM // 2, axis=1) * sin
            o_ref[:, c * HEAD_DIM:(c + 1) * HEAD_DIM] = rot.astype(BF16)
    else:
        o_ref[...] = y.astype(BF16)


def _inproj(h, g, w, cos, sin, col0, width, rotary, tm=1024, tn=1536):
    n, d = h.shape
    assert width % tn == 0 and col0 % tn == 0 and n % tm == 0
    jb = col0 // tn
    return pl.pallas_call(
        functools.partial(_inproj_kernel, rotary=rotary, tn=tn),
        grid=(n // tm, width // tn),
        in_specs=[
            pl.BlockSpec((tm, d), lambda i, j: (i, 0)),
            pl.BlockSpec((1, d), lambda i, j: (0, 0)),
            pl.BlockSpec((d, tn), lambda i, j: (0, jb + j)),
            pl.BlockSpec((tm, HEAD_DIM), lambda i, j: (i, 0)),
            pl.BlockSpec((tm, HEAD_DIM), lambda i, j: (i, 0)),
        ],
        out_specs=pl.BlockSpec((tm, tn), lambda i, j: (i, j)),
        out_shape=jax.ShapeDtypeStruct((n, width), BF16),
        scratch_shapes=[pltpu.VMEM((tm, d), BF16)],
        compiler_params=_params("parallel", "arbitrary"),
        name="norm_inproj_rotary" if rotary else "norm_inproj_plain",
    )(h, g, w, cos, sin)


def _attn_kernel(q_ref, k_ref, v_ref, o_ref, nat, q16, k16, v16, acc, m_s, l_s, *, seq):
    blk = KEYS_BACK
    n_cls = ATTN_BASE_DILATION
    cls_len = seq // n_cls
    scale = HEAD_DIM ** -0.5 * math.log2(math.e)
    last = len(DILATIONS) - 1
    nt = (((1,), (1,)), ((), ()))
    u = ATTN_INTERLEAVE

    def cls_rows(r):
        return pl.ds(pl.multiple_of(r * cls_len, cls_len), cls_len)

    for src_ref, dst in ((q_ref, q16), (k_ref, k16), (v_ref, v16)):
        nat[...] = src_ref[...].astype(F32)

        def split(r, c, dst=dst):
            dst[cls_rows(r), :] = nat[pl.ds(r, cls_len, stride=n_cls), :]
            return c

        lax.fori_loop(0, n_cls, split, 0)

    for gi, (window, d) in enumerate(DILATIONS):
        assert window // d == blk and (n_cls % d == 0 or d % n_cls == 0)
        n_blk = seq // d // blk
        nb = min(ATTN_TILE_BLOCKS, n_blk)
        tiles_per_class = n_blk // nb
        pieces = max(n_cls // d, 1)
        stride = max(d // n_cls, 1)
        plen = blk // pieces
        idx0 = lax.broadcasted_iota(jnp.int32, (blk, blk), 0)
        idx1 = lax.broadcasted_iota(jnp.int32, (blk, blk), 1)
        pos0 = pieces * (idx0 % plen) + idx0 // plen
        pos1 = pieces * (idx1 % plen) + idx1 // plen
        mask_cur = pos1 <= pos0
        mask_prev = pos1 >= pos0
        mask_both = jnp.concatenate([mask_prev, mask_cur], axis=1)

        def piece_rows(rc, b, c, d=d, plen=plen, stride=stride):
            if stride > 1:
                return pl.ds((rc % n_cls) * cls_len + rc // n_cls + stride * blk * b, blk, stride=stride)
            return pl.ds(pl.multiple_of((d * c + rc) * cls_len + plen * b, 8), plen)

        def load_block(ref, rc, b, pieces=pieces, piece_rows=piece_rows):
            return jnp.concatenate([ref[piece_rows(rc, b, c), :] for c in range(pieces)], axis=0)

        def store_block(ref, rc, b, val, pieces=pieces, plen=plen, piece_rows=piece_rows):
            for c in range(pieces):
                ref[piece_rows(rc, b, c), :] = val[c * plen:(c + 1) * plen, :]

        def load_tile(rc, b0, first, gi=gi, nb=nb, load_block=load_block):
            ko = 0 if first else 1
            t = dict(rc=rc, b0=b0, ko=ko,
                     q=[load_block(q16, rc, b0 + b).astype(BF16) for b in range(nb)],
                     k=[load_block(k16, rc, b0 + b - ko).astype(BF16) for b in range(nb + ko)],
                     v=[load_block(v16, rc, b0 + b - ko).astype(BF16) for b in range(nb + ko)])
            if gi > 0:
                t.update(m=[load_block(m_s, rc, b0 + b) for b in range(nb)],
                         l=[load_block(l_s, rc, b0 + b) for b in range(nb)],
                         a=[load_block(acc, rc, b0 + b) for b in range(nb)])
            return t

        def compute_tile(t, gi=gi, nb=nb, mask_cur=mask_cur, mask_both=mask_both):
            out = []
            for b in range(nb):
                own = b + t["ko"]
                if own > 0:
                    k = jnp.concatenate([t["k"][own - 1], t["k"][own]], axis=0)
                    v = jnp.concatenate([t["v"][own - 1], t["v"][own]], axis=0)
                    mask = mask_both
                else:
                    k, v, mask = t["k"][own], t["v"][own], mask_cur
                s = lax.dot_general(t["q"][b], k, nt, preferred_element_type=F32)
                s = jnp.where(mask, s * scale, NEG_INF)
                m_blk = jnp.max(s, axis=-1, keepdims=True)
                if gi == 0:
                    m_new = jnp.broadcast_to(m_blk, (blk, blk))
                else:
                    m_old = t["m"][b]
                    m_new = jnp.maximum(m_old, m_blk)
                p_parts = [jnp.exp2(s[:, c * blk:(c + 1) * blk] - m_new) for c in range(s.shape[1] // blk)]
                p_sum = p_parts[0] if len(p_parts) == 1 else p_parts[0] + p_parts[1]
                l_new = jnp.broadcast_to(jnp.sum(p_sum, axis=-1, keepdims=True), (blk, blk))
                p = jnp.concatenate([x.astype(BF16) for x in p_parts], axis=1)
                pv = jnp.dot(p, v, preferred_element_type=F32)
                if gi > 0:
                    alpha = jnp.exp2(m_old - m_new)
                    l_new = alpha * t["l"][b] + l_new
                    pv = alpha * t["a"][b] + pv
                if gi == last:
                    pv = pv / l_new
                out.append((pv, m_new, l_new))
            return out

        def run_tiles(tiles, gi=gi, nb=nb, load_tile=load_tile, compute_tile=compute_tile,
                      store_block=store_block):
            loaded = [load_tile(rc, b0, first) for rc, b0, first in tiles]
            results = [compute_tile(t) for t in loaded]
            for t, res in zip(loaded, results):
                for b, (pv, m_new, l_new) in enumerate(res):
                    store_block(acc, t["rc"], t["b0"] + b, pv)
                    if gi < last:
                        store_block(m_s, t["rc"], t["b0"] + b, m_new)
                        store_block(l_s, t["rc"], t["b0"] + b, l_new)

        ug = min(u, d * tiles_per_class)
        if tiles_per_class <= ug:
            classes = ug // tiles_per_class
            assert ug % tiles_per_class == 0 and d % classes == 0

            def body(it, c, run_tiles=run_tiles, nb=nb, classes=classes, tpc=tiles_per_class):
                run_tiles([(it * classes + cl, k * nb, k == 0) for cl in range(classes) for k in range(tpc)])
                return c

            if d == classes:
                body(0, 0)
            else:
                lax.fori_loop(0, d // classes, body, 0)
        else:
            assert d == 1 and tiles_per_class % ug == 0
            run_tiles([(0, k * nb, k == 0) for k in range(ug)])

            def body(it, c, run_tiles=run_tiles, nb=nb, ug=ug):
                run_tiles([(0, (it * ug + k) * nb, False) for k in range(ug)])
                return c

            lax.fori_loop(1, tiles_per_class // ug, body, 0)

    def merge(r, c):
        nat[pl.ds(r, cls_len, stride=n_cls), :] = acc[cls_rows(r), :]
        return c

    lax.fori_loop(0, n_cls, merge, 0)
    o_ref[...] = nat[...].astype(BF16)


def _attention(qk, rest, batch, seq):
    assert all(seq % (d * KEYS_BACK) == 0 for _, d in DILATIONS), "every residue class is whole blocks"
    blockspec = lambda off: pl.BlockSpec((None, seq, HEAD_DIM), lambda b, h: (b, 0, off + h))
    return pl.pallas_call(
        functools.partial(_attn_kernel, seq=seq),
        grid=(batch, N_HEADS),
        in_specs=[blockspec(0), blockspec(N_HEADS), blockspec(0)],
        out_specs=blockspec(0),
        out_shape=jax.ShapeDtypeStruct((batch, seq, ATTN_WIDTH), BF16),
        scratch_shapes=[pltpu.VMEM((seq, HEAD_DIM), F32)] * 7,
        compiler_params=_params("parallel", "parallel"),
        name="dilated_attention",
    )(qk, qk, rest)


def _outproj_kernel(h_ref, a_ref, gb_ref, gc_ref, hv_ref, gch_ref, hvh_ref, cw_ref, wo_ref, o_ref,
                    ext, *, tm, seq):
    i = pl.program_id(0)
    first = (i * tm) % seq == 0
    u = gc_ref[...].astype(F32) * hv_ref[...].astype(F32)
    u_halo = gch_ref[...].astype(F32) * hvh_ref[...].astype(F32)
    ext[0:HALO, :] = jnp.where(first, 0.0, u_halo)
    ext[HALO:HALO + tm, :] = u
    w = cw_ref[...]
    conv = w[CONV_K - 1:CONV_K, :] * u
    for back in range(1, CONV_K):
        conv = conv + w[CONV_K - 1 - back:CONV_K - back, :] * ext[pl.ds(HALO - back, tm), :]
    c = (gb_ref[...].astype(F32) * conv).astype(BF16)
    y = jnp.dot(a_ref[...], wo_ref[0:ATTN_WIDTH, :], preferred_element_type=F32)
    y = y + jnp.dot(c, wo_ref[ATTN_WIDTH:, :], preferred_element_type=F32)
    o_ref[...] = h_ref[...] + y


def _outproj(h, attn, proj, conv_w, w_out, seq, tm=512):
    n, d = h.shape
    assert seq % tm == 0 and tm % HALO == 0, "row tiles must not straddle sequences"
    cb = ATTN_WIDTH // CONV_WIDTH
    hb = tm // HALO
    cur = lambda k: pl.BlockSpec((tm, CONV_WIDTH), lambda i: (i, cb + k))
    halo = lambda k: pl.BlockSpec((HALO, CONV_WIDTH), lambda i: (jnp.maximum(i * hb - 1, 0), cb + k))
    return pl.pallas_call(
        functools.partial(_outproj_kernel, tm=tm, seq=seq),
        grid=(n // tm,),
        in_specs=[
            pl.BlockSpec((tm, d), lambda i: (i, 0)),
            pl.BlockSpec((tm, ATTN_WIDTH), lambda i: (i, 0)),
            cur(0), cur(1), cur(2), halo(1), halo(2),
            pl.BlockSpec((CONV_K, CONV_WIDTH), lambda i: (0, 0)),
            pl.BlockSpec((d, d), lambda i: (0, 0)),
        ],
        out_specs=pl.BlockSpec((tm, d), lambda i: (i, 0)),
        out_shape=jax.ShapeDtypeStruct((n, d), F32),
        scratch_shapes=[pltpu.VMEM((HALO + tm, CONV_WIDTH), F32)],
        compiler_params=_params("parallel"),
        name="conv_outproj_residual",
    )(h, attn, proj, proj, proj, proj, proj, conv_w, w_out)


def _ffn_kernel(h_ref, g_ref, wg_ref, wu_ref, wd_ref, o_ref, xn_ref):
    j = pl.program_id(1)

    @pl.when(j == 0)
    def _():
        h = h_ref[...]
        xn_ref[...] = _rms(h, g_ref[...]).astype(BF16)
        o_ref[...] = h

    x = xn_ref[...]
    gate = jnp.dot(x, wg_ref[...], preferred_element_type=F32)
    up = jnp.dot(x, wu_ref[...], preferred_element_type=F32)
    act = (gate * jax.nn.sigmoid(gate) * up).astype(BF16)
    o_ref[...] += jnp.dot(act, wd_ref[...], preferred_element_type=F32)


def _ffn(h, g, wg, wu, wd, tm=1024, tf=512):
    n, d = h.shape
    f = wg.shape[1]
    assert n % tm == 0 and f % tf == 0
    return pl.pallas_call(
        _ffn_kernel,
        grid=(n // tm, f // tf),
        in_specs=[
            pl.BlockSpec((tm, d), lambda i, j: (i, 0)),
            pl.BlockSpec((1, d), lambda i, j: (0, 0)),
            pl.BlockSpec((d, tf), lambda i, j: (0, j)),
            pl.BlockSpec((d, tf), lambda i, j: (0, j)),
            pl.BlockSpec((tf, d), lambda i, j: (j, 0)),
        ],
        out_specs=pl.BlockSpec((tm, d), lambda i, j: (i, 0)),
        out_shape=jax.ShapeDtypeStruct((n, d), F32),
        scratch_shapes=[pltpu.VMEM((tm, d), BF16)],
        compiler_params=_params("parallel", "arbitrary"),
        name="norm_swiglu_residual",
    )(h, g, wg, wu, wd)


ROUTE_E1, ROUTE_E2, ROUTE_R1, ROUTE_R2, ROUTE_G1, ROUTE_G2 = range(6)


def _pool_router_kernel(h_ref, hh_ref, gm_ref, pw_ref, ps_ref, gf_ref, rw_ref,
                        hn_ref, xn_ref, route_ref, cnt_ref, ext, run, *, tm, seq):
    i = pl.program_id(0)

    @pl.when(i == 0)
    def _():
        run[...] = jnp.zeros_like(run)

    pos0 = (i * tm) % seq
    h = h_ref[...]
    gm = gm_ref[...]
    xn = _rms(h, gm)
    ext[0:HALO, :] = jnp.where(pos0 == 0, 0.0, _rms(hh_ref[...], gm))
    ext[HALO:HALO + tm, :] = xn
    t1 = (pos0 + 1 + lax.broadcasted_iota(jnp.int32, (tm, 1), 0)).astype(F32)
    group = h.shape[1] // len(POOL_WINDOWS)
    outs = []
    for gi, w in enumerate(POOL_WINDOWS):
        cols = slice(gi * group, (gi + 1) * group)
        x_g = xn[:, cols]
        assert w & (w - 1) == 0 and w <= HALO
        s = ext[:, cols]
        shift = 1
        while shift < w:
            s = s + pltpu.roll(s, shift, axis=0)
            shift *= 2
        s = s[HALO:, :]
        mixed = s / jnp.minimum(t1, float(w)) - x_g
        outs.append(jnp.dot(mixed.astype(BF16), pw_ref[gi], preferred_element_type=F32))
    hn = h + jnp.concatenate(outs, axis=1) * ps_ref[...]
    hn_ref[...] = hn

    xn2 = _rms(hn, gf_ref[...])
    xn_ref[...] = _pack_bf16_pairs(xn2)
    x_hi = xn2.astype(BF16)
    x_lo = (xn2 - x_hi.astype(F32)).astype(BF16)
    rw = rw_ref[...]
    w_hi = rw.astype(BF16)
    w_lo = (rw - w_hi.astype(F32)).astype(BF16)
    logits = jnp.dot(x_hi, w_hi, preferred_element_type=F32) + (
        jnp.dot(x_hi, w_lo, preferred_element_type=F32) + jnp.dot(x_lo, w_hi, preferred_element_type=F32))
    lane = lax.broadcasted_iota(jnp.int32, (tm, LANES), 1)
    logits = jnp.where(lane < N_EXPERTS, logits, -jnp.inf)
    m1 = jnp.max(logits, axis=-1, keepdims=True)
    e1 = jnp.min(jnp.where(logits == m1, lane, LANES), axis=-1, keepdims=True)
    rest = jnp.where(lane == e1, -jnp.inf, logits)
    m2 = jnp.max(rest, axis=-1, keepdims=True)
    e2 = jnp.min(jnp.where(rest == m2, lane, LANES), axis=-1, keepdims=True)
    ex = jnp.exp(m2 - m1)
    g1 = 1.0 / (1.0 + ex)
    g2 = ex / (1.0 + ex)

    hit1 = lane == e1
    hit2 = lane == e2
    onehot = jnp.where(hit1 | hit2, 1.0, 0.0)
    tri = (lax.broadcasted_iota(jnp.int32, (tm, tm), 1) < lax.broadcasted_iota(jnp.int32, (tm, tm), 0))
    before = jnp.dot(tri.astype(BF16), onehot.astype(BF16), preferred_element_type=F32) + run[...]
    r1 = jnp.sum(jnp.where(hit1, before, 0.0), axis=-1, keepdims=True)
    r2 = jnp.sum(jnp.where(hit2, before, 0.0), axis=-1, keepdims=True)
    run[...] = run[...] + jnp.sum(onehot, axis=0, keepdims=True)
    cnt_ref[...] = run[...]

    route = jnp.zeros((tm, LANES), F32)
    for slot, val in ((ROUTE_E1, e1.astype(F32)), (ROUTE_E2, e2.astype(F32)), (ROUTE_R1, r1),
                      (ROUTE_R2, r2), (ROUTE_G1, g1), (ROUTE_G2, g2)):
        route = jnp.where(lane == slot, val, route)
    route_ref[...] = route


def _pool_router(h, g_mix, pool_w, pool_scale, g_ffn, router_w, seq, tm=512):
    n, d = h.shape
    assert seq % tm == 0 and tm % HALO == 0, "row tiles must not straddle sequences"
    hb = tm // HALO
    row = pl.BlockSpec((tm, d), lambda i: (i, 0))
    vec = pl.BlockSpec((1, d), lambda i: (0, 0))
    rw = jnp.zeros((d, LANES), F32).at[:, :N_EXPERTS].set(router_w)
    return pl.pallas_call(
        functools.partial(_pool_router_kernel, tm=tm, seq=seq),
        grid=(n // tm,),
        in_specs=[
            row,
            pl.BlockSpec((HALO, d), lambda i: (jnp.maximum(i * hb - 1, 0), 0)),
            vec,
            pl.BlockSpec(pool_w.shape, lambda i: (0, 0, 0)),
            vec, vec,
            pl.BlockSpec((d, LANES), lambda i: (0, 0)),
        ],
        out_specs=[row, pl.BlockSpec((tm, d // 2), lambda i: (i, 0)),
                   pl.BlockSpec((tm, LANES), lambda i: (i, 0)), pl.BlockSpec((1, LANES), lambda i: (0, 0))],
        out_shape=[jax.ShapeDtypeStruct((n, d), F32), jax.ShapeDtypeStruct((n, d // 2), jnp.uint32),
                   jax.ShapeDtypeStruct((n, LANES), F32), jax.ShapeDtypeStruct((1, LANES), F32)],
        scratch_shapes=[pltpu.VMEM((HALO + tm, d), F32), pltpu.VMEM((1, LANES), F32)],
        compiler_params=_params("arbitrary"),
        name="pool_residual_router",
    )(h, h, g_mix, pool_w, pool_scale, g_ffn, rw)


def _row_copy(src_ref, src_row, dst_ref, dst_row, sem):
    return pltpu.make_async_copy(src_ref.at[pl.ds(src_row, 1)], dst_ref.at[pl.ds(dst_row, 1)], sem)


def _dispatch_kernel(d1_ref, d2_ref, x_ref, xs_in_ref, xs_ref, sem, *, tm):
    del xs_in_ref

    def issue(t, c):
        _row_copy(x_ref, t, xs_ref, d1_ref[0, t], sem).start()
        _row_copy(x_ref, t, xs_ref, d2_ref[0, t], sem).start()
        return c

    lax.fori_loop(0, tm, issue, 0, unroll=ROW_DMA_UNROLL)

    def drain(t, c):
        _row_copy(x_ref, 0, xs_ref, 0, sem).wait()
        _row_copy(x_ref, 0, xs_ref, 0, sem).wait()
        return c

    lax.fori_loop(0, tm, drain, 0, unroll=ROW_DMA_UNROLL)


def _dispatch(xn, dest1, dest2, n_rows, tm=512):
    n, w = xn.shape
    idx = pl.BlockSpec((None, 1, tm), lambda i: (i, 0, 0), memory_space=pltpu.SMEM)
    return pl.pallas_call(
        functools.partial(_dispatch_kernel, tm=tm),
        grid=(n // tm,),
        in_specs=[idx, idx, pl.BlockSpec((tm, w), lambda i: (i, 0)), pl.BlockSpec(memory_space=pl.ANY)],
        out_specs=pl.BlockSpec(memory_space=pl.ANY),
        out_shape=jax.ShapeDtypeStruct((n_rows, w), xn.dtype),
        scratch_shapes=[pltpu.SemaphoreType.DMA(())],
        input_output_aliases={3: 0},
        compiler_params=_params("arbitrary"),
        name="expert_dispatch",
    )(dest1.reshape(n // tm, 1, tm), dest2.reshape(n // tm, 1, tm), xn, jnp.zeros((n_rows, w), xn.dtype))


def _moe_kernel(te_ref, tr_ref, x_ref, wg_ref, wu_ref, wd_ref, o_ref, xb_ref):
    del te_ref
    i = pl.program_id(0)
    j = pl.program_id(1)
    rows = tr_ref[i]
    half = x_ref.shape[1]

    @pl.when(j == 0)
    def _():
        o_ref[...] = jnp.zeros_like(o_ref)

    @pl.when((rows > 0) & (j == 0))
    def _():
        lo, hi = _unpack_bf16_pairs(x_ref[...])
        xb_ref[:, :half] = lo
        xb_ref[:, half:] = hi

    def swiglu_top_rows(n_rows):
        r = pl.ds(0, n_rows)
        x = xb_ref[r, :]
        gate = jnp.dot(x, wg_ref[...].astype(BF16), preferred_element_type=F32)
        up = jnp.dot(x, wu_ref[...].astype(BF16), preferred_element_type=F32)
        act = (gate * jax.nn.sigmoid(gate) * up).astype(BF16)
        o_ref[r, :] += jnp.dot(act, wd_ref[...].astype(BF16), preferred_element_type=F32)

    tm = o_ref.shape[0]
    pl.when(rows > tm // 2)(lambda: swiglu_top_rows(tm))
    pl.when((rows > tm // 4) & (rows <= tm // 2))(lambda: swiglu_top_rows(tm // 2))
    pl.when((rows > 0) & (rows <= tm // 4))(lambda: swiglu_top_rows(tm // 4))


def _moe(xs, tile_expert, tile_rows, layer, wg, wu, wd, tm, tf=256):
    p, half = xs.shape
    d = 2 * half
    f = wg.shape[3]
    nj = f // tf
    assert p % tm == 0 and f % tf == 0

    def col(i, j, tr):
        return jnp.where(tr[i] > 0, j, nj - 1)

    grid_spec = pltpu.PrefetchScalarGridSpec(
        num_scalar_prefetch=2,
        grid=(p // tm, nj),
        in_specs=[
            pl.BlockSpec((tm, half), lambda i, j, te, tr: (jnp.where(tr[i] > 0, i, 0), 0)),
            pl.BlockSpec((None, None, d, tf), lambda i, j, te, tr: (layer, te[i], 0, col(i, j, tr))),
            pl.BlockSpec((None, None, d, tf), lambda i, j, te, tr: (layer, te[i], 0, col(i, j, tr))),
            pl.BlockSpec((None, None, tf, d), lambda i, j, te, tr: (layer, te[i], col(i, j, tr), 0)),
        ],
        out_specs=pl.BlockSpec((tm, d), lambda i, j, te, tr: (i, 0)),
        scratch_shapes=[pltpu.VMEM((tm, d), BF16)],
    )
    return pl.pallas_call(
        _moe_kernel,
        grid_spec=grid_spec,
        out_shape=jax.ShapeDtypeStruct((p, d), F32),
        compiler_params=_params("arbitrary", "arbitrary"),
        name="grouped_expert_swiglu",
    )(tile_expert, tile_rows, xs, wg, wu, wd)


def _combine_kernel(d1_ref, d2_ref, n1_ref, n2_ref, h_ref, route_ref, gn_ref, ys_ref, o_ref, b1, b2, sem,
                    *, tm, final_norm):
    i = pl.program_id(0)
    slot = i % 2

    def start_gather(i1_ref, i2_ref, s):
        def issue(t, c):
            _row_copy(ys_ref, i1_ref[0, t], b1.at[s], t, sem.at[s]).start()
            _row_copy(ys_ref, i2_ref[0, t], b2.at[s], t, sem.at[s]).start()
            return c

        lax.fori_loop(0, tm, issue, 0, unroll=ROW_DMA_UNROLL)

    pl.when(i == 0)(lambda: start_gather(d1_ref, d2_ref, 0))
    pl.when(i + 1 < pl.num_programs(0))(lambda: start_gather(n1_ref, n2_ref, 1 - slot))

    def drain(t, c):
        _row_copy(ys_ref, 0, b1.at[slot], 0, sem.at[slot]).wait()
        _row_copy(ys_ref, 0, b2.at[slot], 0, sem.at[slot]).wait()
        return c

    lax.fori_loop(0, tm, drain, 0, unroll=ROW_DMA_UNROLL)

    route = route_ref[...]
    g1 = route[:, ROUTE_G1:ROUTE_G1 + 1]
    g2 = route[:, ROUTE_G2:ROUTE_G2 + 1]
    out = h_ref[...] + (g1 * b1[slot] + g2 * b2[slot])
    if final_norm:
        out = _rms(out, gn_ref[...])
    o_ref[...] = out


def _combine(h, route, dest1, dest2, ys, g_final, final_norm, tm=512):
    n, d = h.shape
    assert n % tm == 0
    steps = n // tm
    idx = pl.BlockSpec((None, 1, tm), lambda i: (i, 0, 0), memory_space=pltpu.SMEM)
    nxt = pl.BlockSpec((None, 1, tm), lambda i: (jnp.minimum(i + 1, steps - 1), 0, 0), memory_space=pltpu.SMEM)
    row = pl.BlockSpec((tm, d), lambda i: (i, 0))
    d1 = dest1.reshape(steps, 1, tm)
    d2 = dest2.reshape(steps, 1, tm)
    return pl.pallas_call(
        functools.partial(_combine_kernel, tm=tm, final_norm=final_norm),
        grid=(steps,),
        in_specs=[idx, idx, nxt, nxt, row, pl.BlockSpec((tm, LANES), lambda i: (i, 0)),
                  pl.BlockSpec((1, d), lambda i: (0, 0)), pl.BlockSpec(memory_space=pl.ANY)],
        out_specs=row,
        out_shape=jax.ShapeDtypeStruct((n, d), F32),
        scratch_shapes=[pltpu.VMEM((2, tm, d), F32), pltpu.VMEM((2, tm, d), F32), pltpu.SemaphoreType.DMA((2,))],
        compiler_params=_params("arbitrary"),
        name="expert_combine_residual",
    )(d1, d2, d1, d2, h, route, g_final, ys)


def _moe_layer(h, route, counts, xn, layer, wg, wu, wd, g_final, final_norm, tm=1024):
    n, d = h.shape
    n_tiles_max = 2 * n // tm + N_EXPERTS
    counts = counts[0, :N_EXPERTS].astype(jnp.int32)
    padded = (counts + tm - 1) // tm * tm
    ends = jnp.cumsum(padded)
    offs = ends - padded
    col = lambda k: route[:, k].astype(jnp.int32)
    dest1 = offs[col(ROUTE_E1)] + col(ROUTE_R1)
    dest2 = offs[col(ROUTE_E2)] + col(ROUTE_R2)
    tile_start = jnp.arange(n_tiles_max, dtype=jnp.int32) * tm
    last_start = jnp.minimum(tile_start, ends[-1] - tm)
    tile_expert = jnp.sum((last_start[:, None] >= ends[None, :]).astype(jnp.int32), axis=1)
    tile_rows = jnp.clip((offs + counts)[tile_expert] - tile_start, 0, tm).astype(jnp.int32)
    xs = _dispatch(xn, dest1, dest2, n_tiles_max * tm)
    ys = _moe(xs, tile_expert, tile_rows, layer, wg, wu, wd, tm)
    return _combine(h, route, dest1, dest2, ys, g_final, final_norm)


def kernel(x, positions, norm_mix_even, w_in, conv_w, w_out, norm_ffn_even, w_gate, w_up, w_down,
           norm_mix_odd, pool_w, pool_scale, norm_ffn_odd, router_w, exp_w_gate, exp_w_up, exp_w_down,
           final_norm):
    batch, seq, d = x.shape
    n = batch * seq
    depth = norm_mix_even.shape[0] + norm_mix_odd.shape[0]
    assert depth % 2 == 0, "the final norm is fused into the last (odd) layer's combine"
    h = x.reshape(n, d)
    cos, sin = _rope_tables(positions)
    g_final = final_norm[None, :]
    for layer in range(depth):
        i = layer // 2
        if layer % 2 == 0:
            g_mix, w_in_b = norm_mix_even[i][None, :], w_in[i].astype(BF16)
            n_rot = 2 * ATTN_WIDTH
            qk = _inproj(h, g_mix, w_in_b, cos, sin, 0, n_rot, True)
            rest = _inproj(h, g_mix, w_in_b, cos, sin, n_rot, w_in_b.shape[1] - n_rot, False)
            attn = _attention(qk.reshape(batch, seq, -1), rest.reshape(batch, seq, -1), batch, seq)
            h = _outproj(h, attn.reshape(n, ATTN_WIDTH), rest, conv_w[i], w_out[i].astype(BF16), seq)
            h = _ffn(h, norm_ffn_even[i][None, :], w_gate[i].astype(BF16), w_up[i].astype(BF16),
                     w_down[i].astype(BF16))
        else:
            h, xn, route, counts = _pool_router(h, norm_mix_odd[i][None, :], pool_w[i].astype(BF16),
                                                pool_scale[i][None, :], norm_ffn_odd[i][None, :],
                                                router_w[i], seq)
            h = _moe_layer(h, route, counts, xn, i, exp_w_gate, exp_w_up, exp_w_down, g_final,
                           layer == depth - 1)
    return h.reshape(batch, seq, d)
```
